```python
import math
import jax, jax.numpy as jnp
from jax import lax
import numpy as np

D_MODEL = 1024
BATCH = 8
SEQ = 8192
DEPTH = 4
DEC_BATCH = 2
DEC_SEQ = 16384
PAST_LEN = 128

GRID_W = 64
N_MIXERS = 4
N_LAYERS_A = (DEPTH + 3) // 4
N_LAYERS_B = (DEPTH + 2) // 4
N_LAYERS_C = (DEPTH + 1) // 4
N_LAYERS_D = DEPTH // 4
Q_BLOCK = 128
EPS = 1e-6
NEG_INF = -1e30
ROPE_THETA = 10000.0
NA_HEADS = 16
NA_HEAD_DIM = 64
NA_KH_MAX = 8
NA_KW = 16
MLA_HEADS = 16
MLA_Q_LORA = 384
MLA_KV_LORA = 256
MLA_NOPE = 64
MLA_ROPE = 32
MLA_V = 64
DIFF_HEADS = 8
DIFF_HEAD_DIM = 64
GQA_Q_HEADS = 16
GQA_KV_HEADS = 4
GQA_HEAD_DIM = 64
D_FF = -(-8 * D_MODEL // (3 * 256)) * 256

kernel_name = 'hybrid_bidir_encoder_two_groups'


def rms_norm(x, g):
    xf = x.astype(jnp.float32)
    y = xf * lax.rsqrt(jnp.mean(xf * xf, axis=-1, keepdims=True) + EPS)
    return (y * g.astype(jnp.float32)).astype(x.dtype)


def rope_tables(pos, dim, dtype):
    inv = ROPE_THETA ** (-jnp.arange(0, dim, 2, dtype=jnp.float32) / dim)
    ang = pos.astype(jnp.float32)[:, None] * inv[None, :]
    ang = jnp.concatenate([ang, ang], axis=-1)
    return jnp.cos(ang).astype(dtype), jnp.sin(ang).astype(dtype)


def apply_rope(x, cos, sin):
    half = x.shape[-1] // 2
    rot = jnp.concatenate([-x[..., half:], x[..., :half]], axis=-1)
    return x * cos[:, None, :] + rot * sin[:, None, :]


def sweep_query_blocks(block_fn, seq_len):
    out = lax.map(block_fn, jnp.arange(seq_len // Q_BLOCK))
    out = jnp.moveaxis(out, 0, 1)
    return out.reshape((out.shape[0], seq_len) + out.shape[3:])


def neighbourhood_attention(h, w_in, qk_g, rpb, w_out):
    B, S, _ = h.shape
    rows = S // GRID_W
    kh = min(NA_KH_MAX, rows)
    qkv = (h @ w_in).reshape(B, S, 3, NA_HEADS, NA_HEAD_DIM)
    q = rms_norm(qkv[:, :, 0], qk_g[0]).reshape(B, rows, GRID_W, NA_HEADS, NA_HEAD_DIM)
    k = rms_norm(qkv[:, :, 1], qk_g[1]).reshape(B, rows, GRID_W, NA_HEADS, NA_HEAD_DIM)
    v = qkv[:, :, 2].reshape(B, rows, GRID_W, NA_HEADS, NA_HEAD_DIM)
    col = jnp.arange(GRID_W)
    col_start = jnp.clip(col - NA_KW // 2, 0, GRID_W - NA_KW)
    col_valid = (col[None, :] >= col_start[:, None]) & (col[None, :] < col_start[:, None] + NA_KW)
    dc_idx = jnp.clip(col[None, :] - col[:, None] + NA_KW - 1, 0, 2 * NA_KW - 2)
    rpb_f = rpb.astype(jnp.float32)
    scale = NA_HEAD_DIM ** -0.5

    def row_step(r):
        r0 = jnp.clip(r - kh // 2, 0, rows - kh)
        q_row = lax.dynamic_index_in_dim(q, r, axis=1, keepdims=False)
        k_blk = lax.dynamic_slice_in_dim(k, r0, kh, axis=1)
        v_blk = lax.dynamic_slice_in_dim(v, r0, kh, axis=1)
        dr_idx = r0 + jnp.arange(kh) - r + NA_KH_MAX - 1
        bias = rpb_f[:, dr_idx[:, None, None], dc_idx[None, :, :]]
        s = jnp.einsum('bqhd,bjkhd->bhqjk', q_row, k_blk).astype(jnp.float32) * scale
        s = s + jnp.transpose(bias, (0, 2, 1, 3))[None]
        s = jnp.where(col_valid[None, None, :, None, :], s, NEG_INF)
        p = jax.nn.softmax(s.reshape(B, NA_HEADS, GRID_W, kh * GRID_W), axis=-1)
        p = p.reshape(B, NA_HEADS, GRID_W, kh, GRID_W).astype(v.dtype)
        return jnp.einsum('bhqjk,bjkhd->bqhd', p, v_blk)

    out = lax.map(row_step, jnp.arange(rows))
    out = jnp.moveaxis(out, 0, 1).reshape(B, S, NA_HEADS * NA_HEAD_DIM)
    return out @ w_out


def latent_attention(h, w_in, q_lat_g, kv_lat_g, w_q_up, w_kv_up, qk_g, w_out):
    B, S, _ = h.shape
    lat = h @ w_in
    q_lat = lat[..., :MLA_Q_LORA]
    kv_lat = lat[..., MLA_Q_LORA:MLA_Q_LORA + MLA_KV_LORA]
    k_rope = lat[..., MLA_Q_LORA + MLA_KV_LORA:]
    q = (rms_norm(q_lat, q_lat_g) @ w_q_up).reshape(B, S, MLA_HEADS, MLA_NOPE + MLA_ROPE)
    kv = (rms_norm(kv_lat, kv_lat_g) @ w_kv_up).reshape(B, S, MLA_HEADS, MLA_NOPE + MLA_V)
    k = jnp.concatenate([kv[..., :MLA_NOPE],
                         jnp.broadcast_to(k_rope[:, :, None, :], (B, S, MLA_HEADS, MLA_ROPE))], axis=-1)
    v = kv[..., MLA_NOPE:]
    q = rms_norm(q, qk_g[0])
    k = rms_norm(k, qk_g[1])
    cos, sin = rope_tables(jnp.arange(S), MLA_ROPE, h.dtype)
    q = jnp.concatenate([q[..., :MLA_NOPE], apply_rope(q[..., MLA_NOPE:], cos, sin)], axis=-1)
    k = jnp.concatenate([k[..., :MLA_NOPE], apply_rope(k[..., MLA_NOPE:], cos, sin)], axis=-1)
    scale = (MLA_NOPE + MLA_ROPE) ** -0.5

    def block(i):
        qb = lax.dynamic_slice_in_dim(q, i * Q_BLOCK, Q_BLOCK, axis=1)
        s = jnp.einsum('bqhd,bkhd->bhqk', qb, k).astype(jnp.float32) * scale
        p = jax.nn.softmax(s, axis=-1).astype(v.dtype)
        return jnp.einsum('bhqk,bkhd->bqhd', p, v)

    out = sweep_query_blocks(block, S)
    return out.reshape(B, S, MLA_HEADS * MLA_V) @ w_out


def differential_attention(h, w_in, qk_g, lam_p, sub_g, w_out, lambda_init):
    B, S, _ = h.shape
    nq = DIFF_HEADS * 2 * DIFF_HEAD_DIM
    qkv = h @ w_in
    q = rms_norm(qkv[..., :nq].reshape(B, S, DIFF_HEADS, 2, DIFF_HEAD_DIM), qk_g[0])
    k = rms_norm(qkv[..., nq:2 * nq].reshape(B, S, DIFF_HEADS, 2, DIFF_HEAD_DIM), qk_g[1])
    v = qkv[..., 2 * nq:].reshape(B, S, DIFF_HEADS, 2 * DIFF_HEAD_DIM)
    lp = lam_p.astype(jnp.float32)
    lam = jnp.exp(jnp.sum(lp[0] * lp[1])) - jnp.exp(jnp.sum(lp[2] * lp[3])) + lambda_init
    slopes = 2.0 ** (-8.0 * jnp.arange(1, DIFF_HEADS + 1, dtype=jnp.float32) / DIFF_HEADS)
    kpos = jnp.arange(S, dtype=jnp.float32)
    scale = DIFF_HEAD_DIM ** -0.5

    def block(i):
        qb = lax.dynamic_slice_in_dim(q, i * Q_BLOCK, Q_BLOCK, axis=1)
        qpos = (i * Q_BLOCK + jnp.arange(Q_BLOCK)).astype(jnp.float32)
        alibi = -slopes[:, None, None] * jnp.abs(qpos[:, None] - kpos[None, :])
        s = jnp.einsum('bqhcd,bkhcd->bchqk', qb, k).astype(jnp.float32) * scale + alibi[None, None]
        p = jax.nn.softmax(s, axis=-1)
        a = (p[:, 0] - lam * p[:, 1]).astype(v.dtype)
        return jnp.einsum('bhqk,bkhe->bqhe', a, v)

    out = sweep_query_blocks(block, S)
    out = rms_norm(out, sub_g) * (1.0 - lambda_init)
    return out.reshape(B, S, DIFF_HEADS * 2 * DIFF_HEAD_DIM) @ w_out


def axial_gqa(h, w_in, qk_g, w_out):
    B, S, _ = h.shape
    nq = GQA_Q_HEADS * GQA_HEAD_DIM
    nkv = GQA_KV_HEADS * GQA_HEAD_DIM
    group = GQA_Q_HEADS // GQA_KV_HEADS
    proj = h @ w_in
    q = rms_norm(proj[..., :nq].reshape(B, S, GQA_Q_HEADS, GQA_HEAD_DIM), qk_g[0])
    k = rms_norm(proj[..., nq:nq + nkv].reshape(B, S, GQA_KV_HEADS, GQA_HEAD_DIM), qk_g[1])
    v = proj[..., nq + nkv:].reshape(B, S, GQA_KV_HEADS, GQA_HEAD_DIM)
    t = jnp.arange(S)
    half = GQA_HEAD_DIM // 2
    cos_r, sin_r = rope_tables(t // GRID_W, half, h.dtype)
    cos_c, sin_c = rope_tables(t % GRID_W, half, h.dtype)

    def axial(x):
        return jnp.concatenate([apply_rope(x[..., :half], cos_r, sin_r),
                                apply_rope(x[..., half:], cos_c, sin_c)], axis=-1)

    q = axial(q).reshape(B, S, GQA_KV_HEADS, group, GQA_HEAD_DIM)
    k = axial(k)
    scale = GQA_HEAD_DIM ** -0.5

    def block(i):
        qb = lax.dynamic_slice_in_dim(q, i * Q_BLOCK, Q_BLOCK, axis=1)
        s = jnp.einsum('bqngd,bknd->bngqk', qb, k).astype(jnp.float32) * scale
        p = jax.nn.softmax(s, axis=-1).astype(v.dtype)
        return jnp.einsum('bngqk,bknd->bqngd', p, v)

    out = sweep_query_blocks(block, S)
    return out.reshape(B, S, nq) @ w_out


def swiglu(h, w_in, w_out):
    gu = h @ w_in
    return (jax.nn.silu(gu[..., :D_FF]) * gu[..., D_FF:]) @ w_out


def encoder_trunk(x, c, p):
    c_act = jax.nn.silu(c)
    for l in range(DEPTH):
        mod = c_act @ p['ada_w'][l] + p['ada_b'][l]
        sh1, sc1, g1, sh2, sc2, g2 = [m[:, None, :] for m in jnp.split(mod, 6, axis=-1)]
        h = rms_norm(x, p['norm_g'][l, 0]) * (1.0 + sc1) + sh1
        kind, j = l % N_MIXERS, l // N_MIXERS
        if kind == 0:
            y = neighbourhood_attention(h, p['na_w_in'][j], p['na_qk_g'][j], p['na_rpb'][j], p['na_w_out'][j])
        elif kind == 1:
            y = latent_attention(h, p['mla_w_in'][j], p['mla_q_lat_g'][j], p['mla_kv_lat_g'][j],
                                 p['mla_w_q_up'][j], p['mla_w_kv_up'][j], p['mla_qk_g'][j], p['mla_w_out'][j])
        elif kind == 2:
            lambda_init = 0.8 - 0.6 * math.exp(-0.3 * l)
            y = differential_attention(h, p['diff_w_in'][j], p['diff_qk_g'][j], p['diff_lambda'][j],
                                       p['diff_sub_g'][j], p['diff_w_out'][j], lambda_init)
        else:
            y = axial_gqa(h, p['gqa_w_in'][j], p['gqa_qk_g'][j], p['gqa_w_out'][j])
        x = x + g1 * y
        h = rms_norm(x, p['norm_g'][l, 1]) * (1.0 + sc2) + sh2
        x = x + g2 * swiglu(h, p['ffn_w_in'][l], p['ffn_w_out'][l])
    return x


def setup_inputs(seed: int = 0) -> dict:
    key = jax.random.key(seed)
    ks = iter(jax.random.split(key, 40))
    D = D_MODEL

    def nrm(shape, scale):
        return jax.random.normal(next(ks), shape, jnp.float32) * scale

    def gain(shape):
        return 1.0 + nrm(shape, 0.02)

    gate_offset = jnp.concatenate([jnp.zeros((2 * D,), jnp.float32), jnp.ones((D,), jnp.float32),
                                   jnp.zeros((2 * D,), jnp.float32), jnp.ones((D,), jnp.float32)])
    na_w = NA_HEADS * NA_HEAD_DIM
    mla_in = MLA_Q_LORA + MLA_KV_LORA + MLA_ROPE
    diff_w = DIFF_HEADS * 2 * DIFF_HEAD_DIM
    gqa_in = (GQA_Q_HEADS + 2 * GQA_KV_HEADS) * GQA_HEAD_DIM
    return {
        'x_prompt': nrm((BATCH, SEQ, D), 1.0),
        'x_sample': nrm((DEC_BATCH, DEC_SEQ, D), 1.0),
        'c_prompt': nrm((BATCH, D), 1.0),
        'c_sample': nrm((DEC_BATCH, D), 1.0),
        'norm_g': gain((DEPTH, 2, D)),
        'ada_w': nrm((DEPTH, D, 6 * D), 0.1 * D ** -0.5),
        'ada_b': nrm((DEPTH, 6 * D), 0.02) + gate_offset[None, :],
        'na_w_in': nrm((N_LAYERS_A, D, 3 * na_w), D ** -0.5),
        'na_qk_g': gain((N_LAYERS_A, 2, NA_HEAD_DIM)),
        'na_rpb': nrm((N_LAYERS_A, NA_HEADS, 2 * NA_KH_MAX - 1, 2 * NA_KW - 1), 0.05),
        'na_w_out': nrm((N_LAYERS_A, na_w, D), na_w ** -0.5),
        'mla_w_in': nrm((N_LAYERS_B, D, mla_in), D ** -0.5),
        'mla_q_lat_g': gain((N_LAYERS_B, MLA_Q_LORA)),
        'mla_kv_lat_g': gain((N_LAYERS_B, MLA_KV_LORA)),
        'mla_w_q_up': nrm((N_LAYERS_B, MLA_Q_LORA, MLA_HEADS * (MLA_NOPE + MLA_ROPE)), MLA_Q_LORA ** -0.5),
        'mla_w_kv_up': nrm((N_LAYERS_B, MLA_KV_LORA, MLA_HEADS * (MLA_NOPE + MLA_V)), MLA_KV_LORA ** -0.5),
        'mla_qk_g': gain((N_LAYERS_B, 2, MLA_NOPE + MLA_ROPE)),
        'mla_w_out': nrm((N_LAYERS_B, MLA_HEADS * MLA_V, D), (MLA_HEADS * MLA_V) ** -0.5),
        'diff_w_in': nrm((N_LAYERS_C, D, 3 * diff_w), D ** -0.5),
        'diff_qk_g': gain((N_LAYERS_C, 2, DIFF_HEAD_DIM)),
        'diff_lambda': nrm((N_LAYERS_C, 4, DIFF_HEAD_DIM), 0.1),
        'diff_sub_g': gain((N_LAYERS_C, 2 * DIFF_HEAD_DIM)),
        'diff_w_out': nrm((N_LAYERS_C, diff_w, D), diff_w ** -0.5),
        'gqa_w_in': nrm((N_LAYERS_D, D, gqa_in), D ** -0.5),
        'gqa_qk_g': gain((N_LAYERS_D, 2, GQA_HEAD_DIM)),
        'gqa_w_out': nrm((N_LAYERS_D, GQA_Q_HEADS * GQA_HEAD_DIM, D), (GQA_Q_HEADS * GQA_HEAD_DIM) ** -0.5),
        'ffn_w_in': nrm((DEPTH, D, 2 * D_FF), D ** -0.5),
        'ffn_w_out': nrm((DEPTH, D_FF, D), D_FF ** -0.5),
    }


def reference(x_prompt, x_sample, c_prompt, c_sample, norm_g, ada_w, ada_b,
              na_w_in, na_qk_g, na_rpb, na_w_out,
              mla_w_in, mla_q_lat_g, mla_kv_lat_g, mla_w_q_up, mla_w_kv_up, mla_qk_g, mla_w_out,
              diff_w_in, diff_qk_g, diff_lambda, diff_sub_g, diff_w_out,
              gqa_w_in, gqa_qk_g, gqa_w_out, ffn_w_in, ffn_w_out):
    params = {
        'norm_g': norm_g, 'ada_w': ada_w, 'ada_b': ada_b,
        'na_w_in': na_w_in, 'na_qk_g': na_qk_g, 'na_rpb': na_rpb, 'na_w_out': na_w_out,
        'mla_w_in': mla_w_in, 'mla_q_lat_g': mla_q_lat_g, 'mla_kv_lat_g': mla_kv_lat_g,
        'mla_w_q_up': mla_w_q_up, 'mla_w_kv_up': mla_w_kv_up, 'mla_qk_g': mla_qk_g, 'mla_w_out': mla_w_out,
        'diff_w_in': diff_w_in, 'diff_qk_g': diff_qk_g, 'diff_lambda': diff_lambda,
        'diff_sub_g': diff_sub_g, 'diff_w_out': diff_w_out,
        'gqa_w_in': gqa_w_in, 'gqa_qk_g': gqa_qk_g, 'gqa_w_out': gqa_w_out,
        'ffn_w_in': ffn_w_in, 'ffn_w_out': ffn_w_out,
    }
    y_prompt = encoder_trunk(x_prompt, c_prompt, params)
    y_sample = encoder_trunk(x_sample, c_sample, params)
    return (y_prompt, y_sample)
```

```python
import functools
import math

import numpy as np
import jax
import jax.numpy as jnp
from jax import lax
from jax.experimental import pallas as pl
from jax.experimental.pallas import tpu as pltpu

D_MODEL = 1024
DEPTH = 4
GRID_W = 64
EPS = 1e-6
NEG_INF = -1e30
ROPE_THETA = 10000.0
NA_HEADS, NA_HEAD_DIM, NA_KH, NA_KW = 16, 64, 8, 16
MLA_HEADS, MLA_Q_LORA, MLA_KV_LORA, MLA_NOPE, MLA_ROPE, MLA_V = 16, 384, 256, 64, 32, 64
DIFF_HEADS, DIFF_HEAD_DIM = 8, 64
GQA_Q_HEADS, GQA_KV_HEADS, GQA_HEAD_DIM = 16, 4, 64
D_FF = -(-8 * D_MODEL // (3 * 256)) * 256

LANES = 128
MXU_N = 256
LOG2E = math.log2(math.e)
VMEM_LIMIT = 56 * 1024 * 1024

F32 = jnp.float32
BF16 = jnp.bfloat16


def _cparams(sem):
    return pltpu.CompilerParams(dimension_semantics=sem, vmem_limit_bytes=VMEM_LIMIT)


def _const_spec(shape):
    nd = len(shape)
    return pl.BlockSpec(shape, lambda *_: (0,) * nd)


def _dot(a, b):
    return jnp.dot(a, b, preferred_element_type=F32)


def _dot_nt(a, b):
    return lax.dot_general(a, b, (((1,), (1,)), ((), ())), preferred_element_type=F32)


def _ada_kernel(c_ref, w_ref, b_ref, o_ref):
    c = c_ref[...]
    ca = c * (1.0 / (1.0 + jnp.exp(-c)))
    o_ref[0] = jnp.dot(ca, w_ref[0], preferred_element_type=F32,
                       precision=lax.Precision.HIGHEST) + b_ref[0]


def _ada_mod(c_all, ada_w, ada_b):
    rows = c_all.shape[0]
    depth, d, n = ada_w.shape
    tn = 1536
    return pl.pallas_call(
        _ada_kernel,
        grid=(depth, n // tn),
        in_specs=[pl.BlockSpec((rows, d), lambda l, j: (0, 0)),
                  pl.BlockSpec((1, d, tn), lambda l, j: (l, 0, j)),
                  pl.BlockSpec((1, 1, tn), lambda l, j: (l, 0, j))],
        out_specs=pl.BlockSpec((1, rows, tn), lambda l, j: (l, 0, j)),
        out_shape=jax.ShapeDtypeStruct((depth, rows, n), F32),
        compiler_params=_cparams(("arbitrary", "arbitrary")),
        name="ada_mod",
    )(c_all, ada_w, ada_b.reshape(depth, 1, n))


def _norm_mod(x, g, shift, scale):
    ms = jnp.mean(x * x, axis=-1, keepdims=True)
    y = x * lax.rsqrt(ms + EPS)
    return (y * g) * (1.0 + scale) + shift


def _group_rms(a, gmat, inv_n):
    sq = a * a
    hi = sq.astype(BF16)
    lo = (sq - hi.astype(F32)).astype(BF16)
    ssq = _dot(hi, gmat) + _dot(lo, gmat)
    return a * lax.rsqrt(ssq * inv_n + EPS)


def _rope_chunk(a, cos, sin_signed, first_half):
    fwd = pltpu.roll(a, 16, 1)
    bwd = pltpu.roll(a, LANES - 16, 1)
    return a * cos + jnp.where(first_half, bwd, fwd) * sin_signed


def _first_half_mask():
    lane = lax.broadcasted_iota(jnp.int32, (1, LANES), 1)
    return (lane & 16) == 0


def _qkv_kernel(*refs, n_q, n_k, n_v, q_const, rope):
    if rope:
        (x_ref, mod_ref, ng_ref, w_ref, gq_ref, gk_ref, gm_ref, cos_ref, sin_ref,
         q_ref, k_ref, v_ref) = refs
    else:
        (x_ref, mod_ref, ng_ref, w_ref, gq_ref, gk_ref, gm_ref,
         q_ref, k_ref, v_ref) = refs
    h = _norm_mod(x_ref[0], ng_ref[...], mod_ref[0, 0:1, :], mod_ref[0, 1:2, :])
    hb = h.astype(BF16)
    gmat = gm_ref[...]
    if rope:
        cos = cos_ref[...]
        sin = sin_ref[...]
        fh = _first_half_mask()

    def normed(col0, gain, const):
        a = _dot(hb, w_ref[:, col0:col0 + MXU_N])
        a = _group_rms(a, gmat, 1.0 / 64.0) * (gain * const)
        if rope:
            a = jnp.concatenate(
                [_rope_chunk(a[:, j * LANES:(j + 1) * LANES], cos, sin, fh)
                 for j in range(MXU_N // LANES)], axis=1)
        return a.astype(BF16)

    for c in range(n_q // MXU_N):
        sl = slice(c * MXU_N, (c + 1) * MXU_N)
        q_ref[0, :, sl] = normed(c * MXU_N, gq_ref[:, sl], q_const)
    for c in range(n_k // MXU_N):
        sl = slice(c * MXU_N, (c + 1) * MXU_N)
        k_ref[0, :, sl] = normed(n_q + c * MXU_N, gk_ref[:, sl], 1.0)
    for c in range(n_v // MXU_N):
        sl = slice(c * MXU_N, (c + 1) * MXU_N)
        v_ref[0, :, sl] = _dot(hb, w_ref[:, n_q + n_k + c * MXU_N:
                                          n_q + n_k + (c + 1) * MXU_N]).astype(BF16)


def _qkv_proj(x, mod, ng, w, gq, gk, gmat, n_q, n_k, n_v, q_const, rope_tabs=None, tm=512):
    b, s, d = x.shape
    n = n_q + n_k + n_v
    rope = rope_tabs is not None
    in_specs = [pl.BlockSpec((1, tm, d), lambda i, j: (i, j, 0)),
                pl.BlockSpec((1, 6, d), lambda i, j: (i, 0, 0)),
                _const_spec((1, d)),
                _const_spec((d, n)),
                _const_spec((1, n_q)),
                _const_spec((1, n_k)),
                _const_spec((MXU_N, MXU_N))]
    args = [x, mod, ng, w, gq, gk, gmat]
    if rope:
        in_specs += [pl.BlockSpec((tm, LANES), lambda i, j: (j, 0))] * 2
        args += list(rope_tabs)
    outs = [jax.ShapeDtypeStruct((b, s, m), BF16) for m in (n_q, n_k, n_v)]
    out_specs = [pl.BlockSpec((1, tm, m), lambda i, j: (i, j, 0)) for m in (n_q, n_k, n_v)]
    return pl.pallas_call(
        functools.partial(_qkv_kernel, n_q=n_q, n_k=n_k, n_v=n_v, q_const=q_const, rope=rope),
        grid=(b, s // tm),
        in_specs=in_specs,
        out_specs=out_specs,
        out_shape=outs,
        compiler_params=_cparams(("parallel", "parallel")),
        name="qkv_proj",
    )(*args)


MLA_LAT_PAD = 768
MLA_SLOT = 128


def _mla_proj_kernel(x_ref, mod_ref, ng_ref, win_ref, gql_ref, gkvl_ref, wq_ref, wk_ref, wv_ref,
                     gq_ref, gk_ref, gm_ref, cos_ref, sin_ref, q_ref, k_ref, v_ref, *, q_const):
    h = _norm_mod(x_ref[0], ng_ref[...], mod_ref[0, 0:1, :], mod_ref[0, 1:2, :])
    lat = _dot(h.astype(BF16), win_ref[...])
    q_lat = lat[:, :MLA_Q_LORA]
    kv_lat = lat[:, MLA_Q_LORA:MLA_Q_LORA + MLA_KV_LORA]
    k_rope = lat[:, MLA_Q_LORA + MLA_KV_LORA:]
    qn = q_lat * lax.rsqrt(jnp.mean(q_lat * q_lat, axis=-1, keepdims=True) + EPS) * gql_ref[...]
    kvn = kv_lat * lax.rsqrt(jnp.mean(kv_lat * kv_lat, axis=-1, keepdims=True) + EPS) * gkvl_ref[...]
    qnb = qn.astype(BF16)
    kvnb = kvn.astype(BF16)
    kvr = jnp.concatenate([kvnb, k_rope.astype(BF16)], axis=1)
    gmat = gm_ref[...]
    cos = cos_ref[...]
    sin = sin_ref[...]
    fh = _first_half_mask()
    inv_n = 1.0 / (MLA_NOPE + MLA_ROPE)
    n_slots = MLA_HEADS * MLA_SLOT

    def finish(a, gain, const):
        a = _group_rms(a, gmat, inv_n) * (gain * const)
        a = jnp.concatenate(
            [_rope_chunk(a[:, j * LANES:(j + 1) * LANES], cos, sin, fh)
             for j in range(MXU_N // LANES)], axis=1)
        return a.astype(BF16)

    for c in range(n_slots // MXU_N):
        sl = slice(c * MXU_N, (c + 1) * MXU_N)
        q_ref[0, :, sl] = finish(_dot(qnb, wq_ref[:, sl]), gq_ref[...], q_const)
        k_ref[0, :, sl] = finish(_dot(kvr, wk_ref[:, sl]), gk_ref[...], 1.0)
    for c in range(MLA_HEADS * MLA_V // MXU_N):
        sl = slice(c * MXU_N, (c + 1) * MXU_N)
        v_ref[0, :, sl] = _dot(kvnb, wv_ref[:, sl]).astype(BF16)


def _mla_proj(x, mod, ng, win, gql, gkvl, wq, wk, wv, gq, gk, gmat, cos, sin, q_const, tm=512):
    b, s, d = x.shape
    n_slots = MLA_HEADS * MLA_SLOT
    n_v = MLA_HEADS * MLA_V
    in_specs = [pl.BlockSpec((1, tm, d), lambda i, j: (i, j, 0)),
                pl.BlockSpec((1, 6, d), lambda i, j: (i, 0, 0)),
                _const_spec((1, d)),
                _const_spec(win.shape),
                _const_spec(gql.shape),
                _const_spec(gkvl.shape),
                _const_spec(wq.shape),
                _const_spec(wk.shape),
                _const_spec(wv.shape),
                _const_spec(gq.shape),
                _const_spec(gk.shape),
                _const_spec((MXU_N, MXU_N)),
                pl.BlockSpec((tm, LANES), lambda i, j: (j, 0)),
                pl.BlockSpec((tm, LANES), lambda i, j: (j, 0))]
    outs = [jax.ShapeDtypeStruct((b, s, m), BF16) for m in (n_slots, n_slots, n_v)]
    out_specs = [pl.BlockSpec((1, tm, m), lambda i, j: (i, j, 0)) for m in (n_slots, n_slots, n_v)]
    return pl.pallas_call(
        functools.partial(_mla_proj_kernel, q_const=q_const),
        grid=(b, s // tm),
        in_specs=in_specs,
        out_specs=out_specs,
        out_shape=outs,
        compiler_params=_cparams(("parallel", "parallel")),
        name="mla_proj",
    )(x, mod, ng, win, gql, gkvl, wq, wk, wv, gq, gk, gmat, cos, sin)


def _lo_mask():
    return lax.broadcasted_iota(jnp.int32, (1, LANES), 1) < 64


def _online_softmax(s, m_ref, l_ref, idx):
    tk = s.shape[1]
    m_prev = m_ref[idx]
    m_next = jnp.maximum(m_prev, jnp.max(s, axis=1, keepdims=True))
    alpha = jnp.exp2(m_prev - m_next)
    p = jnp.exp2(s - jnp.concatenate([m_next] * (tk // LANES), axis=1))
    l_ref[idx] = alpha * l_ref[idx] + jnp.sum(p, axis=1, keepdims=True)
    m_ref[idx] = m_next
    return p.astype(BF16), alpha


def _init_state(m_ref, l_ref, acc_ref):
    m_ref[...] = jnp.full(m_ref.shape, NEG_INF, F32)
    l_ref[...] = jnp.zeros(l_ref.shape, F32)
    acc_ref[...] = jnp.zeros(acc_ref.shape, F32)


def _gqa_flash_kernel(q_ref, k_ref, v_ref, o_ref, m_ref, l_ref, acc_ref, *, tq, tkc, group):
    kj = pl.program_id(3)

    @pl.when(kj == 0)
    def _():
        _init_state(m_ref, l_ref, acc_ref)

    lo = _lo_mask()
    q = q_ref[0]
    zero = jnp.zeros((), BF16)
    qs = [jnp.concatenate([jnp.where(msk, q[:, g * LANES:(g + 1) * LANES], zero)
                           for g in range(group)], axis=0)
          for msk in (lo, jnp.logical_not(lo))]

    def chunk(c, carry):
        r0 = pl.multiple_of(c * tkc, tkc)
        k = k_ref[0, pl.ds(r0, tkc), :]
        v = v_ref[0, pl.ds(r0, tkc), :]
        p0, a0 = _online_softmax(_dot_nt(qs[0], k), m_ref, l_ref, 0)
        p1, a1 = _online_softmax(_dot_nt(qs[1], k), m_ref, l_ref, 1)
        pv = _dot(p0, jnp.where(lo, v, zero)) + _dot(p1, jnp.where(lo, zero, v))
        acc_ref[...] = acc_ref[...] * jnp.where(lo, a0, a1) + pv
        return carry

    lax.fori_loop(0, k_ref.shape[1] // tkc, chunk, 0)

    @pl.when(kj == pl.num_programs(3) - 1)
    def _():
        o = acc_ref[...] / jnp.where(lo, l_ref[0], l_ref[1])
        for g in range(group):
            o_ref[0, :, g * LANES:(g + 1) * LANES] = o[g * tq:(g + 1) * tq].astype(BF16)


def _gqa_flash(q, k, v, tq=256, tk=2048, tkc=512):
    b, s, nq = q.shape
    pairs = k.shape[2] // LANES
    group = nq // k.shape[2]
    wq = group * LANES
    tk = min(tk, s)
    m = group * tq
    return pl.pallas_call(
        functools.partial(_gqa_flash_kernel, tq=tq, tkc=tkc, group=group),
        grid=(b, pairs, s // tq, s // tk),
        in_specs=[pl.BlockSpec((1, tq, wq), lambda i, p, a, c: (i, a, p)),
                  pl.BlockSpec((1, tk, LANES), lambda i, p, a, c: (i, c, p)),
                  pl.BlockSpec((1, tk, LANES), lambda i, p, a, c: (i, c, p))],
        out_specs=pl.BlockSpec((1, tq, wq), lambda i, p, a, c: (i, a, p)),
        out_shape=jax.ShapeDtypeStruct((b, s, nq), BF16),
        scratch_shapes=[pltpu.VMEM((2, m, LANES), F32), pltpu.VMEM((2, m, LANES), F32),
                        pltpu.VMEM((m, LANES), F32)],
        compiler_params=_cparams(("parallel", "parallel", "parallel", "arbitrary")),
        name="gqa_flash",
    )(q, k, v)


def _mla_flash_kernel(q_ref, k_ref, v_ref, o_ref, m_ref, l_ref, acc_ref, *, tkc):
    kj = pl.program_id(3)

    @pl.when(kj == 0)
    def _():
        _init_state(m_ref, l_ref, acc_ref)

    lo = _lo_mask()
    zero = jnp.zeros((), BF16)
    q0 = q_ref[0, :, :LANES]
    q1 = q_ref[0, :, LANES:]

    def chunk(c, carry):
        r0 = pl.multiple_of(c * tkc, tkc)
        k = k_ref[0, pl.ds(r0, tkc), :]
        v = v_ref[0, pl.ds(r0, tkc), :]
        p0, a0 = _online_softmax(_dot_nt(q0, k[:, :LANES]), m_ref, l_ref, 0)
        p1, a1 = _online_softmax(_dot_nt(q1, k[:, LANES:]), m_ref, l_ref, 1)
        pv = _dot(p0, jnp.where(lo, v, zero)) + _dot(p1, jnp.where(lo, zero, v))
        acc_ref[...] = acc_ref[...] * jnp.where(lo, a0, a1) + pv
        return carry

    lax.fori_loop(0, k_ref.shape[1] // tkc, chunk, 0)

    @pl.when(kj == pl.num_programs(3) - 1)
    def _():
        o_ref[0] = (acc_ref[...] / jnp.where(lo, l_ref[0], l_ref[1])).astype(BF16)


def _mla_flash(q, k, v, tq=512, tk=2048, tkc=512):
    b, s, nv = v.shape
    pairs = nv // LANES
    tk = min(tk, s)
    return pl.pallas_call(
        functools.partial(_mla_flash_kernel, tkc=tkc),
        grid=(b, pairs, s // tq, s // tk),
        in_specs=[pl.BlockSpec((1, tq, 2 * LANES), lambda i, p, a, c: (i, a, p)),
                  pl.BlockSpec((1, tk, 2 * LANES), lambda i, p, a, c: (i, c, p)),
                  pl.BlockSpec((1, tk, LANES), lambda i, p, a, c: (i, c, p))],
        out_specs=pl.BlockSpec((1, tq, LANES), lambda i, p, a, c: (i, a, p)),
        out_shape=jax.ShapeDtypeStruct((b, s, nv), BF16),
        scratch_shapes=[pltpu.VMEM((2, tq, LANES), F32), pltpu.VMEM((2, tq, LANES), F32),
                        pltpu.VMEM((tq, LANES), F32)],
        compiler_params=_cparams(("parallel", "parallel", "parallel", "arbitrary")),
        name="mla_flash",
    )(q, k, v)


def _diff_flash_kernel(q_ref, k_ref, v_ref, lam_ref, subg_ref, o_ref, m_ref, l_ref, acc_ref,
                       *, tq, tkc, lambda_init):
    h = pl.program_id(1)
    qi = pl.program_id(2)
    kj = pl.program_id(3)
    tk = k_ref.shape[1]

    @pl.when(kj == 0)
    def _():
        _init_state(m_ref, l_ref, acc_ref)

    lo = _lo_mask()
    zero = jnp.zeros((), BF16)
    q = q_ref[0]
    qs = jnp.concatenate([jnp.where(lo, q, zero), jnp.where(lo, zero, q)], axis=0)
    slope = jnp.exp2(-(8.0 / DIFF_HEADS) * (h + 1).astype(F32) * jnp.ones((1, 1), F32)) * LOG2E
    rel = (lax.broadcasted_iota(jnp.int32, (tq, tkc), 0)
           - lax.broadcasted_iota(jnp.int32, (tq, tkc), 1))

    def chunk(c, carry):
        r0 = pl.multiple_of(c * tkc, tkc)
        k = k_ref[0, pl.ds(r0, tkc), :]
        v = v_ref[0, pl.ds(r0, tkc), :]
        dist = jnp.abs(rel + (qi * tq - kj * tk - c * tkc)).astype(F32)
        bias = dist * slope
        s = _dot_nt(qs, k) - jnp.concatenate([bias, bias], axis=0)
        p, a = _online_softmax(s, m_ref, l_ref, slice(None))
        acc_ref[...] = acc_ref[...] * a + _dot(p, v)
        return carry

    lax.fori_loop(0, tk // tkc, chunk, 0)

    @pl.when(kj == pl.num_programs(3) - 1)
    def _():
        lp = lam_ref[...]
        lam = (jnp.exp(jnp.sum(lp[0:1] * lp[1:2], axis=1, keepdims=True))
               - jnp.exp(jnp.sum(lp[2:3] * lp[3:4], axis=1, keepdims=True)) + lambda_init)
        o = acc_ref[...] / l_ref[...]
        o = o[:tq] - lam * o[tq:]
        o = o * lax.rsqrt(jnp.mean(o * o, axis=-1, keepdims=True) + EPS) * subg_ref[...]
        o_ref[0] = (o * (1.0 - lambda_init)).astype(BF16)


def _diff_flash(q, k, v, lam_p, sub_g, lambda_init, tq=512, tk=2048, tkc=512):
    b, s, n = v.shape
    heads = n // LANES
    tk = min(tk, s)
    return pl.pallas_call(
        functools.partial(_diff_flash_kernel, tq=tq, tkc=tkc, lambda_init=lambda_init),
        grid=(b, heads, s // tq, s // tk),
        in_specs=[pl.BlockSpec((1, tq, LANES), lambda i, p, a, c: (i, a, p)),
                  pl.BlockSpec((1, tk, LANES), lambda i, p, a, c: (i, c, p)),
                  pl.BlockSpec((1, tk, LANES), lambda i, p, a, c: (i, c, p)),
                  _const_spec(lam_p.shape),
                  _const_spec(sub_g.shape)],
        out_specs=pl.BlockSpec((1, tq, LANES), lambda i, p, a, c: (i, a, p)),
        out_shape=jax.ShapeDtypeStruct((b, s, n), BF16),
        scratch_shapes=[pltpu.VMEM((2 * tq, LANES), F32), pltpu.VMEM((2 * tq, LANES), F32),
                        pltpu.VMEM((2 * tq, LANES), F32)],
        compiler_params=_cparams(("parallel", "parallel", "parallel", "arbitrary")),
        name="diff_flash",
    )(q, k, v, lam_p, sub_g)


NA_ROWS_PER_STEP = 8
NA_BLOCK = NA_ROWS_PER_STEP * GRID_W
NA_WIN = NA_KH * GRID_W


def _na_kernel(q_ref, kp_ref, kc_ref, kn_ref, vp_ref, vc_ref, vn_ref, bias_ref, o_ref,
               kw_ref, vw_ref, *, rows):
    blk = pl.program_id(2)
    kw_ref[0:NA_BLOCK] = kp_ref[0]
    kw_ref[NA_BLOCK:2 * NA_BLOCK] = kc_ref[0]
    kw_ref[2 * NA_BLOCK:] = kn_ref[0]
    vw_ref[0:NA_BLOCK] = vp_ref[0]
    vw_ref[NA_BLOCK:2 * NA_BLOCK] = vc_ref[0]
    vw_ref[2 * NA_BLOCK:] = vn_ref[0]
    lo = _lo_mask()
    zero = jnp.zeros((), BF16)
    for rl in range(NA_ROWS_PER_STEP):
        r = blk * NA_ROWS_PER_STEP + rl
        r0 = jnp.clip(r - NA_KH // 2, 0, rows - NA_KH)
        start = pl.multiple_of((r0 - blk * NA_ROWS_PER_STEP + NA_ROWS_PER_STEP) * GRID_W, GRID_W)
        off = r0 - r + (NA_KH - 1)
        kwin = kw_ref[pl.ds(start, NA_WIN), :]
        vwin = vw_ref[pl.ds(start, NA_WIN), :]
        qr = q_ref[0, rl * GRID_W:(rl + 1) * GRID_W, :]
        out = None
        for half, msk in enumerate((lo, jnp.logical_not(lo))):
            bias = jnp.concatenate([bias_ref[half, off + 2 * j] for j in range(NA_KH // 2)], axis=1)
            s = _dot_nt(jnp.where(msk, qr, zero), kwin) + bias
            m = jnp.max(s, axis=1, keepdims=True)
            p = jnp.exp(s - m)
            l = jnp.sum(p, axis=1, keepdims=True)
            o = _dot(p.astype(BF16), jnp.where(msk, vwin, zero)) / l
            out = o if out is None else out + o
        o_ref[0, rl * GRID_W:(rl + 1) * GRID_W, :] = out.astype(BF16)


def _na_attention(q, k, v, bias_tab):
    b, s, n = q.shape
    pairs = n // LANES
    rows = s // GRID_W
    nblk = s // NA_BLOCK
    blk_spec = lambda f: pl.BlockSpec((1, NA_BLOCK, LANES), f)
    prev = lambda i, p, a: (i, jnp.maximum(a - 1, 0), p)
    cur = lambda i, p, a: (i, a, p)
    nxt = lambda i, p, a: (i, jnp.minimum(a + 1, nblk - 1), p)
    return pl.pallas_call(
        functools.partial(_na_kernel, rows=rows),
        grid=(b, pairs, nblk),
        in_specs=[blk_spec(cur), blk_spec(prev), blk_spec(cur), blk_spec(nxt),
                  blk_spec(prev), blk_spec(cur), blk_spec(nxt),
                  pl.BlockSpec((2,) + bias_tab.shape[1:], lambda i, p, a: (p, 0, 0, 0))],
        out_specs=blk_spec(cur),
        out_shape=jax.ShapeDtypeStruct((b, s, n), BF16),
        scratch_shapes=[pltpu.VMEM((3 * NA_BLOCK, LANES), BF16),
                        pltpu.VMEM((3 * NA_BLOCK, LANES), BF16)],
        compiler_params=_cparams(("parallel", "parallel", "parallel")),
        name="na_attention",
    )(q, k, k, k, v, v, v, bias_tab)


def _na_bias_table(rpb):
    col = np.arange(GRID_W)
    col_start = np.clip(col - NA_KW // 2, 0, GRID_W - NA_KW)
    valid = (col[None, :] >= col_start[:, None]) & (col[None, :] < col_start[:, None] + NA_KW)
    dc = np.clip(col[None, :] - col[:, None] + NA_KW - 1, 0, 2 * NA_KW - 2)
    t = rpb.astype(F32)[:, :, dc]
    t = jnp.where(jnp.asarray(valid)[None, None], t, NEG_INF)
    return jnp.concatenate([t[:, :-1], t[:, 1:]], axis=-1)


FF_CHUNK = 256


def _post_kernel(x_ref, a_ref, mod_ref, wo_ref, ng_ref, w1_ref, w2_ref, o_ref):
    x = x_ref[0]
    x = x + mod_ref[0, 2:3, :] * _dot(a_ref[0], wo_ref[...])
    h = _norm_mod(x, ng_ref[...], mod_ref[0, 3:4, :], mod_ref[0, 4:5, :]).astype(BF16)
    y = None
    for c in range(D_FF // FF_CHUNK):
        g = _dot(h, w1_ref[:, c * FF_CHUNK:(c + 1) * FF_CHUNK])
        u = _dot(h, w1_ref[:, D_FF + c * FF_CHUNK:D_FF + (c + 1) * FF_CHUNK])
        act = (g * (1.0 / (1.0 + jnp.exp(-g))) * u).astype(BF16)
        part = _dot(act, w2_ref[c * FF_CHUNK:(c + 1) * FF_CHUNK, :])
        y = part if y is None else y + part
    o_ref[0] = x + mod_ref[0, 5:6, :] * y


def _post(x, attn, mod, wo, ng, w1, w2, tm=512):
    b, s, d = x.shape
    single = pl.Buffered(1)
    return pl.pallas_call(
        _post_kernel,
        grid=(b, s // tm),
        in_specs=[pl.BlockSpec((1, tm, d), lambda i, j: (i, j, 0)),
                  pl.BlockSpec((1, tm, attn.shape[2]), lambda i, j: (i, j, 0)),
                  pl.BlockSpec((1, 6, d), lambda i, j: (i, 0, 0)),
                  pl.BlockSpec(wo.shape, lambda i, j: (0, 0), pipeline_mode=single),
                  _const_spec((1, d)),
                  pl.BlockSpec(w1.shape, lambda i, j: (0, 0), pipeline_mode=single),
                  pl.BlockSpec(w2.shape, lambda i, j: (0, 0), pipeline_mode=single)],
        out_specs=pl.BlockSpec((1, tm, d), lambda i, j: (i, j, 0)),
        out_shape=jax.ShapeDtypeStruct((b, s, d), F32),
        compiler_params=_cparams(("parallel", "parallel")),
        name="post_ffn",
    )(x, attn, mod, wo, ng, w1, w2)


def _block_diag_ones(block):
    idx = np.arange(MXU_N) // block
    return jnp.asarray(idx[:, None] == idx[None, :], dtype=BF16)


def _rope_angles(pos, dim):
    inv = ROPE_THETA ** (-jnp.arange(0, dim, 2, dtype=F32) / dim)
    ang = pos.astype(F32)[:, None] * inv[None, :]
    return jnp.concatenate([ang, ang], axis=-1)


def _sin_signed(ang):
    half = ang.shape[-1] // 2
    sign = jnp.concatenate([-jnp.ones((half,), F32), jnp.ones((half,), F32)])
    return jnp.sin(ang) * sign


def _axial_tables(s):
    t = jnp.arange(s)
    half = GQA_HEAD_DIM // 2
    ar, ac = _rope_angles(t // GRID_W, half), _rope_angles(t % GRID_W, half)
    cos = jnp.concatenate([jnp.cos(ar), jnp.cos(ac)], axis=-1)
    sin = jnp.concatenate([_sin_signed(ar), _sin_signed(ac)], axis=-1)
    return jnp.tile(cos, (1, 2)), jnp.tile(sin, (1, 2))


def _mla_tables(s):
    ang = _rope_angles(jnp.arange(s), MLA_ROPE)
    ones = jnp.ones((s, MLA_NOPE), F32)
    zpad = jnp.zeros((s, MLA_SLOT - MLA_NOPE - MLA_ROPE), F32)
    cos = jnp.concatenate([ones, jnp.cos(ang), zpad], axis=-1)
    sin = jnp.concatenate([jnp.zeros_like(ones), _sin_signed(ang), zpad], axis=-1)
    return cos, sin


def _gqa_q_order():
    group = GQA_Q_HEADS // GQA_KV_HEADS
    order = []
    for p in range(GQA_KV_HEADS // 2):
        for g in range(group):
            order += [group * (2 * p) + g, group * (2 * p + 1) + g]
    return np.asarray(order)


def _head_cols(order, width):
    return (np.asarray(order)[:, None] * width + np.arange(width)[None, :]).reshape(-1)


def _mla_weights(w_in, w_q_up, w_kv_up):
    d = w_in.shape[0]
    win = jnp.concatenate([w_in, jnp.zeros((d, MLA_LAT_PAD - w_in.shape[1]), w_in.dtype)], axis=1)
    qd = MLA_NOPE + MLA_ROPE
    wq = w_q_up.reshape(MLA_Q_LORA, MLA_HEADS, qd)
    wq = jnp.concatenate([wq, jnp.zeros((MLA_Q_LORA, MLA_HEADS, MLA_SLOT - qd), wq.dtype)], axis=-1)
    wq = wq.reshape(MLA_Q_LORA, MLA_HEADS * MLA_SLOT)
    wkv = w_kv_up.reshape(MLA_KV_LORA, MLA_HEADS, MLA_NOPE + MLA_V)
    k_nope = jnp.concatenate(
        [wkv[..., :MLA_NOPE], jnp.zeros((MLA_KV_LORA, MLA_HEADS, MLA_SLOT - MLA_NOPE), wkv.dtype)], axis=-1)
    place = np.zeros((LANES, MLA_HEADS, MLA_SLOT), np.float32)
    for i in range(MLA_ROPE):
        place[i, :, MLA_NOPE + i] = 1.0
    wk = jnp.concatenate([k_nope.reshape(MLA_KV_LORA, -1),
                          jnp.asarray(place.reshape(LANES, -1), wkv.dtype)], axis=0)
    wv = wkv[..., MLA_NOPE:].reshape(MLA_KV_LORA, MLA_HEADS * MLA_V)
    return win.astype(BF16), wq.astype(BF16), wk.astype(BF16), wv.astype(BF16)


def _slot_gain(g):
    g = jnp.concatenate([g, jnp.zeros((MLA_SLOT - g.shape[0],), g.dtype)])
    return jnp.tile(g, MXU_N // MLA_SLOT)[None, :]


def _trunk(x, mods, p):
    s = x.shape[1]
    for l in range(DEPTH):
        mod = mods[l]
        ng1 = p['norm_g'][l, 0][None, :]
        ng2 = p['norm_g'][l, 1][None, :]
        kind, j = l % 4, l // 4
        if kind == 0:
            n = NA_HEADS * NA_HEAD_DIM
            q, k, v = _qkv_proj(x, mod, ng1, p['na_w_in'][j],
                                jnp.tile(p['na_qk_g'][j, 0], NA_HEADS)[None, :],
                                jnp.tile(p['na_qk_g'][j, 1], NA_HEADS)[None, :],
                                p['gm64'], n, n, n, NA_HEAD_DIM ** -0.5)
            a = _na_attention(q, k, v, p['na_bias'][j])
            wo = p['na_w_out'][j]
        elif kind == 1:
            win, wq, wk, wv = p['mla_w'][j]
            cos, sin = p['mla_tabs'][s]
            q, k, v = _mla_proj(x, mod, ng1, win, p['mla_q_lat_g'][j][None, :],
                                p['mla_kv_lat_g'][j][None, :], wq, wk, wv,
                                _slot_gain(p['mla_qk_g'][j, 0]), _slot_gain(p['mla_qk_g'][j, 1]),
                                p['gm128'], cos, sin, (MLA_NOPE + MLA_ROPE) ** -0.5 * LOG2E)
            a = _mla_flash(q, k, v)
            wo = p['mla_w_out'][j]
        elif kind == 2:
            n = DIFF_HEADS * 2 * DIFF_HEAD_DIM
            lambda_init = 0.8 - 0.6 * math.exp(-0.3 * l)
            q, k, v = _qkv_proj(x, mod, ng1, p['diff_w_in'][j],
                                jnp.tile(p['diff_qk_g'][j, 0], 2 * DIFF_HEADS)[None, :],
                                jnp.tile(p['diff_qk_g'][j, 1], 2 * DIFF_HEADS)[None, :],
                                p['gm64'], n, n, n, DIFF_HEAD_DIM ** -0.5 * LOG2E)
            a = _diff_flash(q, k, v, p['diff_lambda'][j], p['diff_sub_g'][j][None, :], lambda_init)
            wo = p['diff_w_out'][j]
        else:
            nq = GQA_Q_HEADS * GQA_HEAD_DIM
            nkv = GQA_KV_HEADS * GQA_HEAD_DIM
            q, k, v = _qkv_proj(x, mod, ng1, p['gqa_w_in'][j],
                                jnp.tile(p['gqa_qk_g'][j, 0], GQA_Q_HEADS)[None, :],
                                jnp.tile(p['gqa_qk_g'][j, 1], GQA_KV_HEADS)[None, :],
                                p['gm64'], nq, nkv, nkv, GQA_HEAD_DIM ** -0.5 * LOG2E,
                                rope_tabs=p['gqa_tabs'][s])
            a = _gqa_flash(q, k, v)
            wo = p['gqa_w_out'][j]
        x = _post(x, a, mod, wo, ng2, p['ffn_w_in'][l], p['ffn_w_out'][l])
    return x


def kernel(x_prompt, x_sample, c_prompt, c_sample, norm_g, ada_w, ada_b, na_w_in, na_qk_g, na_rpb, na_w_out, mla_w_in, mla_q_lat_g, mla_kv_lat_g, mla_w_q_up, mla_w_kv_up, mla_qk_g, mla_w_out, diff_w_in, diff_qk_g, diff_lambda, diff_sub_g, diff_w_out, gqa_w_in, gqa_qk_g, gqa_w_out, ffn_w_in, ffn_w_out):
    bp, bs = x_prompt.shape[0], x_sample.shape[0]
    d = x_prompt.shape[2]
    rows = -(-(bp + bs) // 8) * 8
    c_all = jnp.concatenate([c_prompt, c_sample, jnp.zeros((rows - bp - bs, d), F32)], axis=0)
    mods = _ada_mod(c_all, ada_w, ada_b)
    mods_p = mods[:, :bp].reshape(DEPTH, bp, 6, d)
    mods_s = mods[:, bp:bp + bs].reshape(DEPTH, bs, 6, d)

    q_cols = _head_cols(_gqa_q_order(), GQA_HEAD_DIM)
    nq = GQA_Q_HEADS * GQA_HEAD_DIM
    gqa_in = jnp.concatenate([gqa_w_in[:, :, :nq][:, :, q_cols], gqa_w_in[:, :, nq:]], axis=2)
    seqs = sorted({x_prompt.shape[1], x_sample.shape[1]})
    p = {
        'norm_g': norm_g,
        'gm64': _block_diag_ones(64), 'gm128': _block_diag_ones(128),
        'na_w_in': na_w_in.astype(BF16), 'na_qk_g': na_qk_g, 'na_w_out': na_w_out.astype(BF16),
        'na_bias': [_na_bias_table(na_rpb[j]) for j in range(na_rpb.shape[0])],
        'mla_w': [_mla_weights(mla_w_in[j], mla_w_q_up[j], mla_w_kv_up[j])
                  for j in range(mla_w_in.shape[0])],
        'mla_tabs': {s: _mla_tables(s) for s in seqs},
        'mla_q_lat_g': mla_q_lat_g, 'mla_kv_lat_g': mla_kv_lat_g, 'mla_qk_g': mla_qk_g,
        'mla_w_out': mla_w_out.astype(BF16),
        'diff_w_in': diff_w_in.astype(BF16), 'diff_qk_g': diff_qk_g, 'diff_lambda': diff_lambda,
        'diff_sub_g': diff_sub_g, 'diff_w_out': diff_w_out.astype(BF16),
        'gqa_w_in': gqa_in.astype(BF16), 'gqa_qk_g': gqa_qk_g,
        'gqa_w_out': gqa_w_out[:, q_cols, :].astype(BF16),
        'gqa_tabs': {s: _axial_tables(s) for s in seqs},
        'ffn_w_in': ffn_w_in.astype(BF16), 'ffn_w_out': ffn_w_out.astype(BF16),
    }
    return (_trunk(x_prompt, mods_p, p), _trunk(x_sample, mods_s, p))
```

```python
import functools
import math

import numpy as np
import jax
import jax.numpy as jnp
from jax import lax
from jax.experimental import pallas as pl
from jax.experimental.pallas import tpu as pltpu

D_MODEL = 1024
DEPTH = 4
GRID_W = 64
EPS = 1e-6
NEG_INF = -1e30
ROPE_THETA = 10000.0
NA_HEADS, NA_HEAD_DIM, NA_KH, NA_KW = 16, 64, 8, 16
MLA_HEADS, MLA_Q_LORA, MLA_KV_LORA, MLA_NOPE, MLA_ROPE, MLA_V = 16, 384, 256, 64, 32, 64
DIFF_HEADS, DIFF_HEAD_DIM = 8, 64
GQA_Q_HEADS, GQA_KV_HEADS, GQA_HEAD_DIM = 16, 4, 64
D_FF = -(-8 * D_MODEL // (3 * 256)) * 256

LANES = 128
MXU_N = 256
ROT_HALF = 16
assert MLA_ROPE == 2 * ROT_HALF and GQA_HEAD_DIM // 2 == 2 * ROT_HALF
LOG2E = math.log2(math.e)
VMEM_LIMIT = 56 * 1024 * 1024

F32 = jnp.float32
BF16 = jnp.bfloat16


def _cparams(sem):
    return pltpu.CompilerParams(dimension_semantics=sem, vmem_limit_bytes=VMEM_LIMIT)


def _const_spec(shape):
    nd = len(shape)
    return pl.BlockSpec(shape, lambda *_: (0,) * nd)


def _dot(a, b):
    return jnp.dot(a, b, preferred_element_type=F32)


def _dot_nt(a, b):
    return lax.dot_general(a, b, (((1,), (1,)), ((), ())), preferred_element_type=F32)


def _ada_kernel(c_ref, w_ref, b_ref, o_ref):
    c = c_ref[...]
    ca = c * (1.0 / (1.0 + jnp.exp(-c)))
    o_ref[0] = jnp.dot(ca, w_ref[0], preferred_element_type=F32,
                       precision=lax.Precision.HIGHEST) + b_ref[0]


def _ada_mod(c_all, ada_w, ada_b):
    rows = c_all.shape[0]
    depth, d, n = ada_w.shape
    tn = 1536
    return pl.pallas_call(
        _ada_kernel,
        grid=(depth, n // tn),
        in_specs=[pl.BlockSpec((rows, d), lambda l, j: (0, 0)),
                  pl.BlockSpec((1, d, tn), lambda l, j: (l, 0, j)),
                  pl.BlockSpec((1, 1, tn), lambda l, j: (l, 0, j))],
        out_specs=pl.BlockSpec((1, rows, tn), lambda l, j: (l, 0, j)),
        out_shape=jax.ShapeDtypeStruct((depth, rows, n), F32),
        compiler_params=_cparams(("arbitrary", "arbitrary")),
        name="ada_mod",
    )(c_all, ada_w, ada_b.reshape(depth, 1, n))


def _norm_mod(x, g, shift, scale):
    ms = jnp.mean(x * x, axis=-1, keepdims=True)
    y = x * lax.rsqrt(ms + EPS)
    return (y * g) * (1.0 + scale) + shift


def _group_rms(a, gmat, inv_n):
    sq = a * a
    hi = sq.astype(BF16)
    lo = (sq - hi.astype(F32)).astype(BF16)
    ssq = _dot(hi, gmat) + _dot(lo, gmat)
    return a * lax.rsqrt(ssq * inv_n + EPS)


def _rope_chunk(a, cos, sin_signed, first_half):
    fwd = pltpu.roll(a, ROT_HALF, 1)
    bwd = pltpu.roll(a, LANES - ROT_HALF, 1)
    return a * cos + jnp.where(first_half, bwd, fwd) * sin_signed


def _first_half_mask():
    lane = lax.broadcasted_iota(jnp.int32, (1, LANES), 1)
    return (lane & ROT_HALF) == 0


def _qkv_kernel(*refs, n_q, n_k, n_v, q_const, rope):
    if rope:
        (x_ref, mod_ref, ng_ref, w_ref, gq_ref, gk_ref, gm_ref, cos_ref, sin_ref,
         q_ref, k_ref, v_ref) = refs
    else:
        (x_ref, mod_ref, ng_ref, w_ref, gq_ref, gk_ref, gm_ref,
         q_ref, k_ref, v_ref) = refs
    h = _norm_mod(x_ref[0], ng_ref[...], mod_ref[0, 0:1, :], mod_ref[0, 1:2, :])
    hb = h.astype(BF16)
    gmat = gm_ref[...]
    if rope:
        cos = cos_ref[...]
        sin = sin_ref[...]
        fh = _first_half_mask()

    def normed(col0, gain, const):
        a = _dot(hb, w_ref[:, col0:col0 + MXU_N])
        a = _group_rms(a, gmat, 1.0 / 64.0) * (gain * const)
        if rope:
            a = jnp.concatenate(
                [_rope_chunk(a[:, j * LANES:(j + 1) * LANES], cos, sin, fh)
                 for j in range(MXU_N // LANES)], axis=1)
        return a.astype(BF16)

    for c in range(n_q // MXU_N):
        sl = slice(c * MXU_N, (c + 1) * MXU_N)
        q_ref[0, :, sl] = normed(c * MXU_N, gq_ref[:, sl], q_const)
    for c in range(n_k // MXU_N):
        sl = slice(c * MXU_N, (c + 1) * MXU_N)
        k_ref[0, :, sl] = normed(n_q + c * MXU_N, gk_ref[:, sl], 1.0)
    for c in range(n_v // MXU_N):
        sl = slice(c * MXU_N, (c + 1) * MXU_N)
        v_ref[0, :, sl] = _dot(hb, w_ref[:, n_q + n_k + c * MXU_N:
                                          n_q + n_k + (c + 1) * MXU_N]).astype(BF16)


def _qkv_proj(x, mod, ng, w, gq, gk, gmat, n_q, n_k, n_v, q_const, rope_tabs=None, tm=512):
    b, s, d = x.shape
    n = n_q + n_k + n_v
    rope = rope_tabs is not None
    in_specs = [pl.BlockSpec((1, tm, d), lambda i, j: (i, j, 0)),
                pl.BlockSpec((1, 6, d), lambda i, j: (i, 0, 0)),
                _const_spec((1, d)),
                _const_spec((d, n)),
                _const_spec((1, n_q)),
                _const_spec((1, n_k)),
                _const_spec((MXU_N, MXU_N))]
    args = [x, mod, ng, w, gq, gk, gmat]
    if rope:
        in_specs += [pl.BlockSpec((tm, LANES), lambda i, j: (j, 0))] * 2
        args += list(rope_tabs)
    outs = [jax.ShapeDtypeStruct((b, s, m), BF16) for m in (n_q, n_k, n_v)]
    out_specs = [pl.BlockSpec((1, tm, m), lambda i, j: (i, j, 0)) for m in (n_q, n_k, n_v)]
    return pl.pallas_call(
        functools.partial(_qkv_kernel, n_q=n_q, n_k=n_k, n_v=n_v, q_const=q_const, rope=rope),
        grid=(b, s // tm),
        in_specs=in_specs,
        out_specs=out_specs,
        out_shape=outs,
        compiler_params=_cparams(("parallel", "parallel")),
        name="qkv_proj",
    )(*args)


MLA_LAT_PAD = 768
MLA_SLOT = 128


def _mla_proj_kernel(x_ref, mod_ref, ng_ref, win_ref, gql_ref, gkvl_ref, wq_ref, wk_ref, wv_ref,
                     gq_ref, gk_ref, gm_ref, cos_ref, sin_ref, q_ref, k_ref, v_ref, *, q_const):
    h = _norm_mod(x_ref[0], ng_ref[...], mod_ref[0, 0:1, :], mod_ref[0, 1:2, :])
    lat = _dot(h.astype(BF16), win_ref[...])
    q_lat = lat[:, :MLA_Q_LORA]
    kv_lat = lat[:, MLA_Q_LORA:MLA_Q_LORA + MLA_KV_LORA]
    k_rope = lat[:, MLA_Q_LORA + MLA_KV_LORA:]
    qn = q_lat * lax.rsqrt(jnp.mean(q_lat * q_lat, axis=-1, keepdims=True) + EPS) * gql_ref[...]
    kvn = kv_lat * lax.rsqrt(jnp.mean(kv_lat * kv_lat, axis=-1, keepdims=True) + EPS) * gkvl_ref[...]
    qnb = qn.astype(BF16)
    kvnb = kvn.astype(BF16)
    kvr = jnp.concatenate([kvnb, k_rope.astype(BF16)], axis=1)
    gmat = gm_ref[...]
    cos = cos_ref[...]
    sin = sin_ref[...]
    fh = _first_half_mask()
    inv_n = 1.0 / (MLA_NOPE + MLA_ROPE)
    n_slots = MLA_HEADS * MLA_SLOT

    def finish(a, gain, const):
        a = _group_rms(a, gmat, inv_n) * (gain * const)
        a = jnp.concatenate(
            [_rope_chunk(a[:, j * LANES:(j + 1) * LANES], cos, sin, fh)
             for j in range(MXU_N // LANES)], axis=1)
        return a.astype(BF16)

    for c in range(n_slots // MXU_N):
        sl = slice(c * MXU_N, (c + 1) * MXU_N)
        q_ref[0, :, sl] = finish(_dot(qnb, wq_ref[:, sl]), gq_ref[...], q_const)
        k_ref[0, :, sl] = finish(_dot(kvr, wk_ref[:, sl]), gk_ref[...], 1.0)
    for c in range(MLA_HEADS * MLA_V // MXU_N):
        sl = slice(c * MXU_N, (c + 1) * MXU_N)
        v_ref[0, :, sl] = _dot(kvnb, wv_ref[:, sl]).astype(BF16)


def _mla_proj(x, mod, ng, win, gql, gkvl, wq, wk, wv, gq, gk, gmat, cos, sin, q_const, tm=512):
    b, s, d = x.shape
    n_slots = MLA_HEADS * MLA_SLOT
    n_v = MLA_HEADS * MLA_V
    in_specs = [pl.BlockSpec((1, tm, d), lambda i, j: (i, j, 0)),
                pl.BlockSpec((1, 6, d), lambda i, j: (i, 0, 0)),
                _const_spec((1, d)),
                _const_spec(win.shape),
                _const_spec(gql.shape),
                _const_spec(gkvl.shape),
                _const_spec(wq.shape),
                _const_spec(wk.shape),
                _const_spec(wv.shape),
                _const_spec(gq.shape),
                _const_spec(gk.shape),
                _const_spec((MXU_N, MXU_N)),
                pl.BlockSpec((tm, LANES), lambda i, j: (j, 0)),
                pl.BlockSpec((tm, LANES), lambda i, j: (j, 0))]
    outs = [jax.ShapeDtypeStruct((b, s, m), BF16) for m in (n_slots, n_slots, n_v)]
    out_specs = [pl.BlockSpec((1, tm, m), lambda i, j: (i, j, 0)) for m in (n_slots, n_slots, n_v)]
    return pl.pallas_call(
        functools.partial(_mla_proj_kernel, q_const=q_const),
        grid=(b, s // tm),
        in_specs=in_specs,
        out_specs=out_specs,
        out_shape=outs,
        compiler_params=_cparams(("parallel", "parallel")),
        name="mla_proj",
    )(x, mod, ng, win, gql, gkvl, wq, wk, wv, gq, gk, gmat, cos, sin)


HEAD_LANES = 64


def _lo_mask():
    return lax.broadcasted_iota(jnp.int32, (1, LANES), 1) < HEAD_LANES


def _chunk_start(c, tkc):
    return c * tkc if isinstance(c, int) else pl.multiple_of(c * tkc, tkc)


def _flash_loop(*, n_chunks, n_streams, scores, pv_rhs, m_ref, acc_ref, p_refs, a_refs):
    lo = _lo_mask()
    m_ref[...] = jnp.full(m_ref.shape, NEG_INF, F32)
    acc_ref[...] = jnp.zeros(acc_ref.shape, F32)
    p_refs[1][...] = jnp.zeros(p_refs[1].shape, BF16)
    a_refs[1][...] = jnp.ones(a_refs[1].shape, F32)

    def softmax_stage(c, slot):
        for h in range(n_streams):
            s = scores(c, h)
            m_prev = m_ref[h]
            m_next = jnp.maximum(m_prev, jnp.max(s, axis=1, keepdims=True))
            a_refs[slot][h] = jnp.exp2(m_prev - m_next)
            m_ref[h] = m_next
            m_rep = jnp.concatenate([m_next] * (s.shape[1] // LANES), axis=1)
            p_refs[slot][h] = jnp.exp2(s - m_rep).astype(BF16)

    def pv_stage(c, slot):
        pv = None
        for h in range(n_streams):
            t = _dot(p_refs[slot][h], pv_rhs(c, h))
            pv = t if pv is None else pv + t
        a = a_refs[slot][0]
        if n_streams == 2:
            a = jnp.where(lo, a, a_refs[slot][1])
        acc_ref[...] = acc_ref[...] * jnp.concatenate([a, a], axis=1) + pv

    def body(j, carry):
        softmax_stage(2 * j, 0)
        pv_stage(jnp.maximum(2 * j - 1, 0), 1)
        softmax_stage(2 * j + 1, 1)
        pv_stage(2 * j, 0)
        return carry

    assert n_chunks % 2 == 0
    lax.fori_loop(0, n_chunks // 2, body, 0)
    pv_stage(n_chunks - 1, 1)


def _pair_pv_rhs(v, h):
    msk = _half_mask(v.shape, h)
    rhs = jnp.concatenate([jnp.where(msk, v.astype(F32), 0.0), jnp.where(msk, 1.0, 0.0)], axis=1)
    return rhs.astype(BF16)


def _half_mask(shape, h):
    lane = lax.broadcasted_iota(jnp.int32, shape, len(shape) - 1)
    return (lane < HEAD_LANES) if h == 0 else (lane >= HEAD_LANES)


def _mask_half(x, h):
    return jnp.where(_half_mask(x.shape, h), x.astype(F32), 0.0).astype(BF16)


def _flash_scratch(n_streams, m, tkc):
    return [pltpu.VMEM((n_streams, m, LANES), F32),
            pltpu.VMEM((m, 2 * LANES), F32),
            pltpu.VMEM((n_streams, m, tkc), BF16),
            pltpu.VMEM((n_streams, m, tkc), BF16),
            pltpu.VMEM((n_streams, m, LANES), F32),
            pltpu.VMEM((n_streams, m, LANES), F32)]


def _gqa_flash_kernel(q_ref, k_ref, v_ref, o_ref, m_ref, acc_ref, p0_ref, p1_ref, a0_ref, a1_ref,
                      *, tq, tkc, group):
    q = q_ref[0]
    qs = [jnp.concatenate([_mask_half(q[:, g * LANES:(g + 1) * LANES], h)
                           for g in range(group)], axis=0)
          for h in range(2)]

    def scores(c, h):
        return _dot_nt(qs[h], k_ref[0, pl.ds(_chunk_start(c, tkc), tkc), :])

    def pv_rhs(c, h):
        return _pair_pv_rhs(v_ref[0, pl.ds(_chunk_start(c, tkc), tkc), :], h)

    _flash_loop(n_chunks=k_ref.shape[1] // tkc, n_streams=2, scores=scores, pv_rhs=pv_rhs,
                m_ref=m_ref, acc_ref=acc_ref, p_refs=(p0_ref, p1_ref), a_refs=(a0_ref, a1_ref))

    o = acc_ref[:, :LANES] / acc_ref[:, LANES:]
    for g in range(group):
        o_ref[0, :, g * LANES:(g + 1) * LANES] = o[g * tq:(g + 1) * tq].astype(BF16)


def _gqa_flash(q, k, v, tq=256, tkc=512):
    b, s, nq = q.shape
    pairs = k.shape[2] // LANES
    group = nq // k.shape[2]
    wq = group * LANES
    tkc = min(tkc, s // 2)
    return pl.pallas_call(
        functools.partial(_gqa_flash_kernel, tq=tq, tkc=tkc, group=group),
        grid=(b, pairs, s // tq),
        in_specs=[pl.BlockSpec((1, tq, wq), lambda i, p, a: (i, a, p)),
                  pl.BlockSpec((1, s, LANES), lambda i, p, a: (i, 0, p)),
                  pl.BlockSpec((1, s, LANES), lambda i, p, a: (i, 0, p))],
        out_specs=pl.BlockSpec((1, tq, wq), lambda i, p, a: (i, a, p)),
        out_shape=jax.ShapeDtypeStruct((b, s, nq), BF16),
        scratch_shapes=_flash_scratch(2, group * tq, tkc),
        compiler_params=_cparams(("parallel", "parallel", "parallel")),
        name="gqa_flash",
    )(q, k, v)


def _mla_flash_kernel(q_ref, k_ref, v_ref, o_ref, m_ref, acc_ref, p0_ref, p1_ref, a0_ref, a1_ref,
                      *, tkc):
    qs = [q_ref[0, :, :LANES], q_ref[0, :, LANES:]]

    def scores(c, h):
        return _dot_nt(qs[h], k_ref[0, pl.ds(_chunk_start(c, tkc), tkc), h * LANES:(h + 1) * LANES])

    def pv_rhs(c, h):
        return _pair_pv_rhs(v_ref[0, pl.ds(_chunk_start(c, tkc), tkc), :], h)

    _flash_loop(n_chunks=k_ref.shape[1] // tkc, n_streams=2, scores=scores, pv_rhs=pv_rhs,
                m_ref=m_ref, acc_ref=acc_ref, p_refs=(p0_ref, p1_ref), a_refs=(a0_ref, a1_ref))
    o_ref[0] = (acc_ref[:, :LANES] / acc_ref[:, LANES:]).astype(BF16)


def _mla_flash(q, k, v, tq=1024, tkc=512):
    b, s, nv = v.shape
    pairs = nv // LANES
    tkc = min(tkc, s // 2)
    return pl.pallas_call(
        functools.partial(_mla_flash_kernel, tkc=tkc),
        grid=(b, pairs, s // tq),
        in_specs=[pl.BlockSpec((1, tq, 2 * LANES), lambda i, p, a: (i, a, p)),
                  pl.BlockSpec((1, s, 2 * LANES), lambda i, p, a: (i, 0, p)),
                  pl.BlockSpec((1, s, LANES), lambda i, p, a: (i, 0, p))],
        out_specs=pl.BlockSpec((1, tq, LANES), lambda i, p, a: (i, a, p)),
        out_shape=jax.ShapeDtypeStruct((b, s, nv), BF16),
        scratch_shapes=_flash_scratch(2, tq, tkc),
        compiler_params=_cparams(("parallel", "parallel", "parallel")),
        name="mla_flash",
    )(q, k, v)


def _diff_flash_kernel(q_ref, k_ref, v_ref, lam_ref, subg_ref, o_ref, m_ref, acc_ref,
                       p0_ref, p1_ref, a0_ref, a1_ref, rel_ref, *, tq, tkc, lambda_init):
    h = pl.program_id(1)
    qi = pl.program_id(2)
    q = q_ref[0]
    qs = jnp.concatenate([_mask_half(q, 0), _mask_half(q, 1)], axis=0)
    slope = jnp.exp2(-(8.0 / DIFF_HEADS) * (h + 1).astype(F32) * jnp.ones((1, 1), F32)) * LOG2E
    rel_ref[...] = (lax.broadcasted_iota(jnp.int32, (tq, tkc), 0)
                    - lax.broadcasted_iota(jnp.int32, (tq, tkc), 1)).astype(F32)

    def scores(c, _):
        off = (qi * tq - c * tkc).astype(F32)
        bias = jnp.abs(rel_ref[...] + off) * slope
        k = k_ref[0, pl.ds(_chunk_start(c, tkc), tkc), :]
        return _dot_nt(qs, k) - jnp.concatenate([bias, bias], axis=0)

    def pv_rhs(c, _):
        v = v_ref[0, pl.ds(_chunk_start(c, tkc), tkc), :]
        return jnp.concatenate([v, jnp.ones(v.shape, BF16)], axis=1)

    _flash_loop(n_chunks=k_ref.shape[1] // tkc, n_streams=1, scores=scores, pv_rhs=pv_rhs,
                m_ref=m_ref, acc_ref=acc_ref, p_refs=(p0_ref, p1_ref), a_refs=(a0_ref, a1_ref))

    lp = lam_ref[...]
    lam = (jnp.exp(jnp.sum(lp[0:1] * lp[1:2], axis=1, keepdims=True))
           - jnp.exp(jnp.sum(lp[2:3] * lp[3:4], axis=1, keepdims=True)) + lambda_init)
    o = acc_ref[:, :LANES] / acc_ref[:, LANES:]
    o = o[:tq] - lam * o[tq:]
    o = o * lax.rsqrt(jnp.mean(o * o, axis=-1, keepdims=True) + EPS) * subg_ref[...]
    o_ref[0] = (o * (1.0 - lambda_init)).astype(BF16)


def _diff_flash(q, k, v, lam_p, sub_g, lambda_init, tq=512, tkc=1024):
    b, s, n = v.shape
    heads = n // LANES
    tkc = min(tkc, s // 2)
    return pl.pallas_call(
        functools.partial(_diff_flash_kernel, tq=tq, tkc=tkc, lambda_init=lambda_init),
        grid=(b, heads, s // tq),
        in_specs=[pl.BlockSpec((1, tq, LANES), lambda i, p, a: (i, a, p)),
                  pl.BlockSpec((1, s, LANES), lambda i, p, a: (i, 0, p)),
                  pl.BlockSpec((1, s, LANES), lambda i, p, a: (i, 0, p)),
                  _const_spec(lam_p.shape),
                  _const_spec(sub_g.shape)],
        out_specs=pl.BlockSpec((1, tq, LANES), lambda i, p, a: (i, a, p)),
        out_shape=jax.ShapeDtypeStruct((b, s, n), BF16),
        scratch_shapes=_flash_scratch(1, 2 * tq, tkc) + [pltpu.VMEM((tq, tkc), F32)],
        compiler_params=_cparams(("parallel", "parallel", "parallel")),
        name="diff_flash",
    )(q, k, v, lam_p, sub_g)


NA_ROWS_PER_STEP = 8
NA_BLOCK = NA_ROWS_PER_STEP * GRID_W
NA_WIN = NA_KH * GRID_W


def _na_kernel(q_ref, kp_ref, kc_ref, kn_ref, vp_ref, vc_ref, vn_ref, bias_ref, o_ref,
               kw_ref, vw_ref, *, rows):
    blk = pl.program_id(2)
    kw_ref[0:NA_BLOCK] = kp_ref[0]
    kw_ref[NA_BLOCK:2 * NA_BLOCK] = kc_ref[0]
    kw_ref[2 * NA_BLOCK:] = kn_ref[0]
    vw_ref[0:NA_BLOCK] = vp_ref[0]
    vw_ref[NA_BLOCK:2 * NA_BLOCK] = vc_ref[0]
    vw_ref[2 * NA_BLOCK:] = vn_ref[0]
    lo = _lo_mask()
    for rl in range(NA_ROWS_PER_STEP):
        r = blk * NA_ROWS_PER_STEP + rl
        r0 = jnp.clip(r - NA_KH // 2, 0, rows - NA_KH)
        start = pl.multiple_of((r0 - blk * NA_ROWS_PER_STEP + NA_ROWS_PER_STEP) * GRID_W, GRID_W)
        off = r0 - r + (NA_KH - 1)
        kwin = kw_ref[pl.ds(start, NA_WIN), :]
        vwin = vw_ref[pl.ds(start, NA_WIN), :]
        qr = q_ref[0, rl * GRID_W:(rl + 1) * GRID_W, :]
        qs = jnp.concatenate([_mask_half(qr, 0), _mask_half(qr, 1)], axis=0)
        bias = jnp.concatenate(
            [jnp.concatenate([bias_ref[half, off + 2 * j] for j in range(NA_KH // 2)], axis=1)
             for half in range(2)], axis=0)
        s = _dot_nt(qs, kwin) + bias
        p = jnp.exp(s - jnp.max(s, axis=1, keepdims=True))
        o = _dot(p.astype(BF16), vwin) / jnp.sum(p, axis=1, keepdims=True)
        o_ref[0, rl * GRID_W:(rl + 1) * GRID_W, :] = jnp.where(lo, o[:GRID_W], o[GRID_W:]).astype(BF16)


def _na_attention(q, k, v, bias_tab):
    b, s, n = q.shape
    pairs = n // LANES
    rows = s // GRID_W
    nblk = s // NA_BLOCK
    blk_spec = lambda f: pl.BlockSpec((1, NA_BLOCK, LANES), f)
    prev = lambda i, p, a: (i, jnp.maximum(a - 1, 0), p)
    cur = lambda i, p, a: (i, a, p)
    nxt = lambda i, p, a: (i, jnp.minimum(a + 1, nblk - 1), p)
    return pl.pallas_call(
        functools.partial(_na_kernel, rows=rows),
        grid=(b, pairs, nblk),
        in_specs=[blk_spec(cur), blk_spec(prev), blk_spec(cur), blk_spec(nxt),
                  blk_spec(prev), blk_spec(cur), blk_spec(nxt),
                  pl.BlockSpec((2,) + bias_tab.shape[1:], lambda i, p, a: (p, 0, 0, 0))],
        out_specs=blk_spec(cur),
        out_shape=jax.ShapeDtypeStruct((b, s, n), BF16),
        scratch_shapes=[pltpu.VMEM((3 * NA_BLOCK, LANES), BF16),
                        pltpu.VMEM((3 * NA_BLOCK, LANES), BF16)],
        compiler_params=_cparams(("parallel", "parallel", "parallel")),
        name="na_attention",
    )(q, k, k, k, v, v, v, bias_tab)


def _na_bias_table(rpb):
    col = np.arange(GRID_W)
    col_start = np.clip(col - NA_KW // 2, 0, GRID_W - NA_KW)
    valid = (col[None, :] >= col_start[:, None]) & (col[None, :] < col_start[:, None] + NA_KW)
    dc = np.clip(col[None, :] - col[:, None] + NA_KW - 1, 0, 2 * NA_KW - 2)
    t = rpb.astype(F32)[:, :, dc]
    t = jnp.where(jnp.asarray(valid)[None, None], t, NEG_INF)
    return jnp.concatenate([t[:, :-1], t[:, 1:]], axis=-1)


FF_CHUNK = 256


def _post_kernel(x_ref, a_ref, mod_ref, wo_ref, ng_ref, w1_ref, w2_ref, o_ref):
    x = x_ref[0]
    x = x + mod_ref[0, 2:3, :] * _dot(a_ref[0], wo_ref[...])
    h = _norm_mod(x, ng_ref[...], mod_ref[0, 3:4, :], mod_ref[0, 4:5, :]).astype(BF16)
    y = None
    for c in range(D_FF // FF_CHUNK):
        g = _dot(h, w1_ref[:, c * FF_CHUNK:(c + 1) * FF_CHUNK])
        u = _dot(h, w1_ref[:, D_FF + c * FF_CHUNK:D_FF + (c + 1) * FF_CHUNK])
        act = (g * (1.0 / (1.0 + jnp.exp(-g))) * u).astype(BF16)
        part = _dot(act, w2_ref[c * FF_CHUNK:(c + 1) * FF_CHUNK, :])
        y = part if y is None else y + part
    o_ref[0] = x + mod_ref[0, 5:6, :] * y


def _post(x, attn, mod, wo, ng, w1, w2, tm=512):
    b, s, d = x.shape
    single = pl.Buffered(1)
    return pl.pallas_call(
        _post_kernel,
        grid=(b, s // tm),
        in_specs=[pl.BlockSpec((1, tm, d), lambda i, j: (i, j, 0)),
                  pl.BlockSpec((1, tm, attn.shape[2]), lambda i, j: (i, j, 0)),
                  pl.BlockSpec((1, 6, d), lambda i, j: (i, 0, 0)),
                  pl.BlockSpec(wo.shape, lambda i, j: (0, 0), pipeline_mode=single),
                  _const_spec((1, d)),
                  pl.BlockSpec(w1.shape, lambda i, j: (0, 0), pipeline_mode=single),
                  pl.BlockSpec(w2.shape, lambda i, j: (0, 0), pipeline_mode=single)],
        out_specs=pl.BlockSpec((1, tm, d), lambda i, j: (i, j, 0)),
        out_shape=jax.ShapeDtypeStruct((b, s, d), F32),
        compiler_params=_cparams(("parallel", "parallel")),
        name="post_ffn",
    )(x, attn, mod, wo, ng, w1, w2)


def _block_diag_ones(block):
    idx = np.arange(MXU_N) // block
    return jnp.asarray(idx[:, None] == idx[None, :], dtype=BF16)


def _rope_angles(pos, dim):
    inv = ROPE_THETA ** (-jnp.arange(0, dim, 2, dtype=F32) / dim)
    ang = pos.astype(F32)[:, None] * inv[None, :]
    return jnp.concatenate([ang, ang], axis=-1)


def _sin_signed(ang):
    half = ang.shape[-1] // 2
    sign = jnp.concatenate([-jnp.ones((half,), F32), jnp.ones((half,), F32)])
    return jnp.sin(ang) * sign


def _axial_tables(s):
    t = jnp.arange(s)
    half = GQA_HEAD_DIM // 2
    ar, ac = _rope_angles(t // GRID_W, half), _rope_angles(t % GRID_W, half)
    cos = jnp.concatenate([jnp.cos(ar), jnp.cos(ac)], axis=-1)
    sin = jnp.concatenate([_sin_signed(ar), _sin_signed(ac)], axis=-1)
    return jnp.tile(cos, (1, 2)), jnp.tile(sin, (1, 2))


def _mla_tables(s):
    ang = _rope_angles(jnp.arange(s), MLA_ROPE)
    ones = jnp.ones((s, MLA_NOPE), F32)
    zpad = jnp.zeros((s, MLA_SLOT - MLA_NOPE - MLA_ROPE), F32)
    cos = jnp.concatenate([ones, jnp.cos(ang), zpad], axis=-1)
    sin = jnp.concatenate([jnp.zeros_like(ones), _sin_signed(ang), zpad], axis=-1)
    return cos, sin


def _gqa_q_order():
    group = GQA_Q_HEADS // GQA_KV_HEADS
    order = []
    for p in range(GQA_KV_HEADS // 2):
        for g in range(group):
            order += [group * (2 * p) + g, group * (2 * p + 1) + g]
    return np.asarray(order)


def _head_cols(order, width):
    return (np.asarray(order)[:, None] * width + np.arange(width)[None, :]).reshape(-1)


def _mla_weights(w_in, w_q_up, w_kv_up):
    d = w_in.shape[0]
    win = jnp.concatenate([w_in, jnp.zeros((d, MLA_LAT_PAD - w_in.shape[1]), w_in.dtype)], axis=1)
    qd = MLA_NOPE + MLA_ROPE
    wq = w_q_up.reshape(MLA_Q_LORA, MLA_HEADS, qd)
    wq = jnp.concatenate([wq, jnp.zeros((MLA_Q_LORA, MLA_HEADS, MLA_SLOT - qd), wq.dtype)], axis=-1)
    wq = wq.reshape(MLA_Q_LORA, MLA_HEADS * MLA_SLOT)
    wkv = w_kv_up.reshape(MLA_KV_LORA, MLA_HEADS, MLA_NOPE + MLA_V)
    k_nope = jnp.concatenate(
        [wkv[..., :MLA_NOPE], jnp.zeros((MLA_KV_LORA, MLA_HEADS, MLA_SLOT - MLA_NOPE), wkv.dtype)], axis=-1)
    place = np.zeros((LANES, MLA_HEADS, MLA_SLOT), np.float32)
    for i in range(MLA_ROPE):
        place[i, :, MLA_NOPE + i] = 1.0
    wk = jnp.concatenate([k_nope.reshape(MLA_KV_LORA, -1),
                          jnp.asarray(place.reshape(LANES, -1), wkv.dtype)], axis=0)
    wv = wkv[..., MLA_NOPE:].reshape(MLA_KV_LORA, MLA_HEADS * MLA_V)
    return win.astype(BF16), wq.astype(BF16), wk.astype(BF16), wv.astype(BF16)


def _slot_gain(g):
    g = jnp.concatenate([g, jnp.zeros((MLA_SLOT - g.shape[0],), g.dtype)])
    return jnp.tile(g, MXU_N // MLA_SLOT)[None, :]


def _trunk(x, mods, p):
    s = x.shape[1]
    for l in range(DEPTH):
        mod = mods[l]
        ng1 = p['norm_g'][l, 0][None, :]
        ng2 = p['norm_g'][l, 1][None, :]
        kind, j = l % 4, l // 4
        if kind == 0:
            n = NA_HEADS * NA_HEAD_DIM
            q, k, v = _qkv_proj(x, mod, ng1, p['na_w_in'][j],
                                jnp.tile(p['na_qk_g'][j, 0], NA_HEADS)[None, :],
                                jnp.tile(p['na_qk_g'][j, 1], NA_HEADS)[None, :],
                                p['gm64'], n, n, n, NA_HEAD_DIM ** -0.5)
            a = _na_attention(q, k, v, p['na_bias'][j])
            wo = p['na_w_out'][j]
        elif kind == 1:
            win, wq, wk, wv = p['mla_w'][j]
            cos, sin = p['mla_tabs'][s]
            q, k, v = _mla_proj(x, mod, ng1, win, p['mla_q_lat_g'][j][None, :],
                                p['mla_kv_lat_g'][j][None, :], wq, wk, wv,
                                _slot_gain(p['mla_qk_g'][j, 0]), _slot_gain(p['mla_qk_g'][j, 1]),
                                p['gm128'], cos, sin, (MLA_NOPE + MLA_ROPE) ** -0.5 * LOG2E)
            a = _mla_flash(q, k, v)
            wo = p['mla_w_out'][j]
        elif kind == 2:
            n = DIFF_HEADS * 2 * DIFF_HEAD_DIM
            lambda_init = 0.8 - 0.6 * math.exp(-0.3 * l)
            q, k, v = _qkv_proj(x, mod, ng1, p['diff_w_in'][j],
                                jnp.tile(p['diff_qk_g'][j, 0], 2 * DIFF_HEADS)[None, :],
                                jnp.tile(p['diff_qk_g'][j, 1], 2 * DIFF_HEADS)[None, :],
                                p['gm64'], n, n, n, DIFF_HEAD_DIM ** -0.5 * LOG2E)
            a = _diff_flash(q, k, v, p['diff_lambda'][j], p['diff_sub_g'][j][None, :], lambda_init)
            wo = p['diff_w_out'][j]
        else:
            nq = GQA_Q_HEADS * GQA_HEAD_DIM
            nkv = GQA_KV_HEADS * GQA_HEAD_DIM
            q, k, v = _qkv_proj(x, mod, ng1, p['gqa_w_in'][j],
                                jnp.tile(p['gqa_qk_g'][j, 0], GQA_Q_HEADS)[None, :],
                                jnp.tile(p['gqa_qk_g'][j, 1], GQA_KV_HEADS)[None, :],
                                p['gm64'], nq, nkv, nkv, GQA_HEAD_DIM ** -0.5 * LOG2E,
                                rope_tabs=p['gqa_tabs'][s])
            a = _gqa_flash(q, k, v)
            wo = p['gqa_w_out'][j]
        x = _post(x, a, mod, wo, ng2, p['ffn_w_in'][l], p['ffn_w_out'][l])
    return x


def kernel(x_prompt, x_sample, c_prompt, c_sample, norm_g, ada_w, ada_b, na_w_in, na_qk_g, na_rpb, na_w_out, mla_w_in, mla_q_lat_g, mla_kv_lat_g, mla_w_q_up, mla_w_kv_up, mla_qk_g, mla_w_out, diff_w_in, diff_qk_g, diff_lambda, diff_sub_g, diff_w_out, gqa_w_in, gqa_qk_g, gqa_w_out, ffn_w_in, ffn_w_out):
    bp, bs = x_prompt.shape[0], x_sample.shape[0]
    d = x_prompt.shape[2]
    rows = -(-(bp + bs) // 8) * 8
    c_all = jnp.concatenate([c_prompt, c_sample, jnp.zeros((rows - bp - bs, d), F32)], axis=0)
    mods = _ada_mod(c_all, ada_w, ada_b)
    mods_p = mods[:, :bp].reshape(DEPTH, bp, 6, d)
    mods_s = mods[:, bp:bp + bs].reshape(DEPTH, bs, 6, d)

    q_cols = _head_cols(_gqa_q_order(), GQA_HEAD_DIM)
    nq = GQA_Q_HEADS * GQA_HEAD_DIM
    gqa_in = jnp.concatenate([gqa_w_in[:, :, :nq][:, :, q_cols], gqa_w_in[:, :, nq:]], axis=2)
    seqs = sorted({x_prompt.shape[1], x_sample.shape[1]})
    p = {
        'norm_g': norm_g,
        'gm64': _block_diag_ones(64), 'gm128': _block_diag_ones(128),
        'na_w_in': na_w_in.astype(BF16), 'na_qk_g': na_qk_g, 'na_w_out': na_w_out.astype(BF16),
        'na_bias': [_na_bias_table(na_rpb[j]) for j in range(na_rpb.shape[0])],
        'mla_w': [_mla_weights(mla_w_in[j], mla_w_q_up[j], mla_w_kv_up[j])
                  for j in range(mla_w_in.shape[0])],
        'mla_tabs': {s: _mla_tables(s) for s in seqs},
        'mla_q_lat_g': mla_q_lat_g, 'mla_kv_lat_g': mla_kv_lat_g, 'mla_qk_g': mla_qk_g,
        'mla_w_out': mla_w_out.astype(BF16),
        'diff_w_in': diff_w_in.astype(BF16), 'diff_qk_g': diff_qk_g, 'diff_lambda': diff_lambda,
        'diff_sub_g': diff_sub_g, 'diff_w_out': diff_w_out.astype(BF16),
        'gqa_w_in': gqa_in.astype(BF16), 'gqa_qk_g': gqa_qk_g,
        'gqa_w_out': gqa_w_out[:, q_cols, :].astype(BF16),
        'gqa_tabs': {s: _axial_tables(s) for s in seqs},
        'ffn_w_in': ffn_w_in.astype(BF16), 'ffn_w_out': ffn_w_out.astype(BF16),
    }
    return (_trunk(x_prompt, mods_p, p), _trunk(x_sample, mods_s, p))
```

```python
import functools
import math

import numpy as np
import jax
import jax.numpy as jnp
from jax import lax
from jax.experimental import pallas as pl
from jax.experimental.pallas import tpu as pltpu

D_MODEL = 1024
DEPTH = 4
GRID_W = 64
EPS = 1e-6
NEG_INF = -1e30
ROPE_THETA = 10000.0
NA_HEADS, NA_HEAD_DIM, NA_KH, NA_KW = 16, 64, 8, 16
MLA_HEADS, MLA_Q_LORA, MLA_KV_LORA, MLA_NOPE, MLA_ROPE, MLA_V = 16, 384, 256, 64, 32, 64
DIFF_HEADS, DIFF_HEAD_DIM = 8, 64
GQA_Q_HEADS, GQA_KV_HEADS, GQA_HEAD_DIM = 16, 4, 64
D_FF = -(-8 * D_MODEL // (3 * 256)) * 256

LANES = 128
MXU_N = 256
ROT_HALF = 16
assert MLA_ROPE == 2 * ROT_HALF and GQA_HEAD_DIM // 2 == 2 * ROT_HALF
LOG2E = math.log2(math.e)
VMEM_LIMIT = 56 * 1024 * 1024

F32 = jnp.float32
BF16 = jnp.bfloat16


def _cparams(sem):
    return pltpu.CompilerParams(dimension_semantics=sem, vmem_limit_bytes=VMEM_LIMIT)


def _const_spec(shape):
    nd = len(shape)
    return pl.BlockSpec(shape, lambda *_: (0,) * nd)


def _dot(a, b):
    return jnp.dot(a, b, preferred_element_type=F32)


def _dot_nt(a, b):
    return lax.dot_general(a, b, (((1,), (1,)), ((), ())), preferred_element_type=F32)


def _ada_kernel(c_ref, w_ref, b_ref, o_ref):
    c = c_ref[...]
    ca = c * (1.0 / (1.0 + jnp.exp(-c)))
    o_ref[0] = jnp.dot(ca, w_ref[0], preferred_element_type=F32,
                       precision=lax.Precision.HIGHEST) + b_ref[0]


def _ada_mod(c_all, ada_w, ada_b):
    rows = c_all.shape[0]
    depth, d, n = ada_w.shape
    tn = 1536
    return pl.pallas_call(
        _ada_kernel,
        grid=(depth, n // tn),
        in_specs=[pl.BlockSpec((rows, d), lambda l, j: (0, 0)),
                  pl.BlockSpec((1, d, tn), lambda l, j: (l, 0, j)),
                  pl.BlockSpec((1, 1, tn), lambda l, j: (l, 0, j))],
        out_specs=pl.BlockSpec((1, rows, tn), lambda l, j: (l, 0, j)),
        out_shape=jax.ShapeDtypeStruct((depth, rows, n), F32),
        compiler_params=_cparams(("arbitrary", "arbitrary")),
        name="ada_mod",
    )(c_all, ada_w, ada_b.reshape(depth, 1, n))


def _norm_mod(x, g, shift, scale):
    ms = jnp.mean(x * x, axis=-1, keepdims=True)
    y = x * lax.rsqrt(ms + EPS)
    return (y * g) * (1.0 + scale) + shift


def _group_rms(a, gmat, inv_n):
    ssq = _dot((a * a).astype(BF16), gmat)
    return a * lax.rsqrt(ssq * inv_n + EPS)


def _rope_chunk(a, cos, sin_signed, first_half):
    fwd = pltpu.roll(a, ROT_HALF, 1)
    bwd = pltpu.roll(a, LANES - ROT_HALF, 1)
    return a * cos + jnp.where(first_half, bwd, fwd) * sin_signed


def _first_half_mask():
    lane = lax.broadcasted_iota(jnp.int32, (1, LANES), 1)
    return (lane & ROT_HALF) == 0


def _qkv_kernel(*refs, n_q, n_k, n_v, q_const, rope):
    if rope:
        (x_ref, mod_ref, ng_ref, w_ref, gq_ref, gk_ref, gm_ref, cos_ref, sin_ref,
         q_ref, k_ref, v_ref) = refs
    else:
        (x_ref, mod_ref, ng_ref, w_ref, gq_ref, gk_ref, gm_ref,
         q_ref, k_ref, v_ref) = refs
    h = _norm_mod(x_ref[0], ng_ref[...], mod_ref[0, 0:1, :], mod_ref[0, 1:2, :])
    hb = h.astype(BF16)
    gmat = gm_ref[...]
    if rope:
        cos = cos_ref[...]
        sin = sin_ref[...]
        fh = _first_half_mask()

    def normed(col0, gain, const):
        a = _dot(hb, w_ref[:, col0:col0 + MXU_N])
        a = _group_rms(a, gmat, 1.0 / 64.0) * (gain * const)
        if rope:
            a = jnp.concatenate(
                [_rope_chunk(a[:, j * LANES:(j + 1) * LANES], cos, sin, fh)
                 for j in range(MXU_N // LANES)], axis=1)
        return a.astype(BF16)

    for c in range(n_q // MXU_N):
        sl = slice(c * MXU_N, (c + 1) * MXU_N)
        q_ref[0, :, sl] = normed(c * MXU_N, gq_ref[:, sl], q_const)
    for c in range(n_k // MXU_N):
        sl = slice(c * MXU_N, (c + 1) * MXU_N)
        k_ref[0, :, sl] = normed(n_q + c * MXU_N, gk_ref[:, sl], 1.0)
    for c in range(n_v // MXU_N):
        sl = slice(c * MXU_N, (c + 1) * MXU_N)
        v_ref[0, :, sl] = _dot(hb, w_ref[:, n_q + n_k + c * MXU_N:
                                          n_q + n_k + (c + 1) * MXU_N]).astype(BF16)


def _qkv_proj(x, mod, ng, w, gq, gk, gmat, n_q, n_k, n_v, q_const, rope_tabs=None, tm=512):
    b, s, d = x.shape
    n = n_q + n_k + n_v
    rope = rope_tabs is not None
    in_specs = [pl.BlockSpec((1, tm, d), lambda i, j: (i, j, 0)),
                pl.BlockSpec((1, 6, d), lambda i, j: (i, 0, 0)),
                _const_spec((1, d)),
                _const_spec((d, n)),
                _const_spec((1, n_q)),
                _const_spec((1, n_k)),
                _const_spec((MXU_N, MXU_N))]
    args = [x, mod, ng, w, gq, gk, gmat]
    if rope:
        in_specs += [pl.BlockSpec((tm, LANES), lambda i, j: (j, 0))] * 2
        args += list(rope_tabs)
    outs = [jax.ShapeDtypeStruct((b, s, m), BF16) for m in (n_q, n_k, n_v)]
    out_specs = [pl.BlockSpec((1, tm, m), lambda i, j: (i, j, 0)) for m in (n_q, n_k, n_v)]
    return pl.pallas_call(
        functools.partial(_qkv_kernel, n_q=n_q, n_k=n_k, n_v=n_v, q_const=q_const, rope=rope),
        grid=(b, s // tm),
        in_specs=in_specs,
        out_specs=out_specs,
        out_shape=outs,
        compiler_params=_cparams(("parallel", "parallel")),
        name="qkv_proj",
    )(*args)


MLA_LAT_PAD = 768
MLA_SLOT = 128


def _mla_proj_kernel(x_ref, mod_ref, ng_ref, win_ref, gql_ref, gkvl_ref, wq_ref, wk_ref, wv_ref,
                     gq_ref, gk_ref, gm_ref, cos_ref, sin_ref, q_ref, k_ref, v_ref, *, q_const):
    h = _norm_mod(x_ref[0], ng_ref[...], mod_ref[0, 0:1, :], mod_ref[0, 1:2, :])
    lat = _dot(h.astype(BF16), win_ref[...])
    q_lat = lat[:, :MLA_Q_LORA]
    kv_lat = lat[:, MLA_Q_LORA:MLA_Q_LORA + MLA_KV_LORA]
    k_rope = lat[:, MLA_Q_LORA + MLA_KV_LORA:]
    qn = q_lat * lax.rsqrt(jnp.mean(q_lat * q_lat, axis=-1, keepdims=True) + EPS) * gql_ref[...]
    kvn = kv_lat * lax.rsqrt(jnp.mean(kv_lat * kv_lat, axis=-1, keepdims=True) + EPS) * gkvl_ref[...]
    qnb = qn.astype(BF16)
    kvnb = kvn.astype(BF16)
    kvr = jnp.concatenate([kvnb, k_rope.astype(BF16)], axis=1)
    gmat = gm_ref[...]
    cos = cos_ref[...]
    sin = sin_ref[...]
    fh = _first_half_mask()
    inv_n = 1.0 / (MLA_NOPE + MLA_ROPE)
    n_slots = MLA_HEADS * MLA_SLOT

    def finish(a, gain, const):
        a = _group_rms(a, gmat, inv_n) * (gain * const)
        a = jnp.concatenate(
            [_rope_chunk(a[:, j * LANES:(j + 1) * LANES], cos, sin, fh)
             for j in range(MXU_N // LANES)], axis=1)
        return a.astype(BF16)

    for c in range(n_slots // MXU_N):
        sl = slice(c * MXU_N, (c + 1) * MXU_N)
        q_ref[0, :, sl] = finish(_dot(qnb, wq_ref[:, sl]), gq_ref[...], q_const)
        k_ref[0, :, sl] = finish(_dot(kvr, wk_ref[:, sl]), gk_ref[...], 1.0)
    for c in range(MLA_HEADS * MLA_V // MXU_N):
        sl = slice(c * MXU_N, (c + 1) * MXU_N)
        v_ref[0, :, sl] = _dot(kvnb, wv_ref[:, sl]).astype(BF16)


def _mla_proj(x, mod, ng, win, gql, gkvl, wq, wk, wv, gq, gk, gmat, cos, sin, q_const, tm=512):
    b, s, d = x.shape
    n_slots = MLA_HEADS * MLA_SLOT
    n_v = MLA_HEADS * MLA_V
    in_specs = [pl.BlockSpec((1, tm, d), lambda i, j: (i, j, 0)),
                pl.BlockSpec((1, 6, d), lambda i, j: (i, 0, 0)),
                _const_spec((1, d)),
                _const_spec(win.shape),
                _const_spec(gql.shape),
                _const_spec(gkvl.shape),
                _const_spec(wq.shape),
                _const_spec(wk.shape),
                _const_spec(wv.shape),
                _const_spec(gq.shape),
                _const_spec(gk.shape),
                _const_spec((MXU_N, MXU_N)),
                pl.BlockSpec((tm, LANES), lambda i, j: (j, 0)),
                pl.BlockSpec((tm, LANES), lambda i, j: (j, 0))]
    outs = [jax.ShapeDtypeStruct((b, s, m), BF16) for m in (n_slots, n_slots, n_v)]
    out_specs = [pl.BlockSpec((1, tm, m), lambda i, j: (i, j, 0)) for m in (n_slots, n_slots, n_v)]
    return pl.pallas_call(
        functools.partial(_mla_proj_kernel, q_const=q_const),
        grid=(b, s // tm),
        in_specs=in_specs,
        out_specs=out_specs,
        out_shape=outs,
        compiler_params=_cparams(("parallel", "parallel")),
        name="mla_proj",
    )(x, mod, ng, win, gql, gkvl, wq, wk, wv, gq, gk, gmat, cos, sin)


HEAD_LANES = 64


def _lo_mask():
    return lax.broadcasted_iota(jnp.int32, (1, LANES), 1) < HEAD_LANES


def _chunk_start(c, tkc):
    return c * tkc if isinstance(c, int) else pl.multiple_of(c * tkc, tkc)


def _flash_loop(*, n_chunks, n_streams, scores, pv_rhs, m_ref, acc_ref, p_refs, a_refs):
    lo = _lo_mask()
    m_ref[...] = jnp.full(m_ref.shape, NEG_INF, F32)
    acc_ref[...] = jnp.zeros(acc_ref.shape, F32)

    def softmax_stage(i, slot):
        for h in range(n_streams):
            s, shift = scores(i, h)
            m_prev = m_ref[h]
            row_max = jnp.max(s, axis=1, keepdims=True)
            if shift is not None:
                row_max = row_max + shift
            m_next = jnp.maximum(m_prev, row_max)
            a_refs[slot][h] = jnp.exp2(m_prev - m_next)
            m_ref[h] = m_next
            m_eff = m_next if shift is None else m_next - shift
            m_rep = jnp.concatenate([m_eff] * (s.shape[1] // LANES), axis=1)
            p_refs[slot][h] = jnp.exp2(s - m_rep).astype(BF16)

    def pv_stage(i, slot):
        pv = None
        for h in range(n_streams):
            t = _dot(p_refs[slot][h], pv_rhs(i, h))
            pv = t if pv is None else pv + t
        a = a_refs[slot][0]
        if n_streams == 2:
            a = jnp.where(lo, a, a_refs[slot][1])
        acc_ref[...] = acc_ref[...] * jnp.concatenate([a, a], axis=1) + pv

    def body(j, carry):
        softmax_stage(2 * j, 0)
        pv_stage(2 * j - 1, 1)
        softmax_stage(2 * j + 1, 1)
        pv_stage(2 * j, 0)
        return carry

    assert n_chunks % 2 == 0
    softmax_stage(0, 0)
    softmax_stage(1, 1)
    pv_stage(0, 0)
    lax.fori_loop(1, n_chunks // 2, body, 0, unroll=2)
    pv_stage(n_chunks - 1, 1)


def _pair_pv_rhs(v, h):
    msk = _half_mask(v.shape, h)
    rhs = jnp.concatenate([jnp.where(msk, v.astype(F32), 0.0), jnp.where(msk, 1.0, 0.0)], axis=1)
    return rhs.astype(BF16)


def _half_mask(shape, h):
    lane = lax.broadcasted_iota(jnp.int32, shape, len(shape) - 1)
    return (lane < HEAD_LANES) if h == 0 else (lane >= HEAD_LANES)


def _mask_half(x, h):
    return jnp.where(_half_mask(x.shape, h), x.astype(F32), 0.0).astype(BF16)


def _flash_scratch(n_streams, m, tkc):
    return [pltpu.VMEM((n_streams, m, LANES), F32),
            pltpu.VMEM((m, 2 * LANES), F32),
            pltpu.VMEM((n_streams, m, tkc), BF16),
            pltpu.VMEM((n_streams, m, tkc), BF16),
            pltpu.VMEM((n_streams, m, LANES), F32),
            pltpu.VMEM((n_streams, m, LANES), F32)]


def _gqa_flash_kernel(q_ref, k_ref, v_ref, o_ref, m_ref, acc_ref, p0_ref, p1_ref, a0_ref, a1_ref,
                      *, tq, tkc, group):
    q = q_ref[0]
    qs = [jnp.concatenate([_mask_half(q[:, g * LANES:(g + 1) * LANES], h)
                           for g in range(group)], axis=0)
          for h in range(2)]

    def scores(c, h):
        return _dot_nt(qs[h], k_ref[0, pl.ds(_chunk_start(c, tkc), tkc), :]), None

    def pv_rhs(c, h):
        return _pair_pv_rhs(v_ref[0, pl.ds(_chunk_start(c, tkc), tkc), :], h)

    _flash_loop(n_chunks=k_ref.shape[1] // tkc, n_streams=2, scores=scores, pv_rhs=pv_rhs,
                m_ref=m_ref, acc_ref=acc_ref, p_refs=(p0_ref, p1_ref), a_refs=(a0_ref, a1_ref))

    o = acc_ref[:, :LANES] / acc_ref[:, LANES:]
    for g in range(group):
        o_ref[0, :, g * LANES:(g + 1) * LANES] = o[g * tq:(g + 1) * tq].astype(BF16)


def _gqa_flash(q, k, v, tq=256, tkc=512):
    b, s, nq = q.shape
    pairs = k.shape[2] // LANES
    group = nq // k.shape[2]
    wq = group * LANES
    tkc = min(tkc, s // 2)
    return pl.pallas_call(
        functools.partial(_gqa_flash_kernel, tq=tq, tkc=tkc, group=group),
        grid=(b, pairs, s // tq),
        in_specs=[pl.BlockSpec((1, tq, wq), lambda i, p, a: (i, a, p)),
                  pl.BlockSpec((1, s, LANES), lambda i, p, a: (i, 0, p)),
                  pl.BlockSpec((1, s, LANES), lambda i, p, a: (i, 0, p))],
        out_specs=pl.BlockSpec((1, tq, wq), lambda i, p, a: (i, a, p)),
        out_shape=jax.ShapeDtypeStruct((b, s, nq), BF16),
        scratch_shapes=_flash_scratch(2, group * tq, tkc),
        compiler_params=_cparams(("parallel", "parallel", "parallel")),
        name="gqa_flash",
    )(q, k, v)


def _mla_flash_kernel(q_ref, k_ref, v_ref, o_ref, m_ref, acc_ref, p0_ref, p1_ref, a0_ref, a1_ref,
                      *, tkc):
    qs = [q_ref[0, :, :LANES], q_ref[0, :, LANES:]]

    def scores(c, h):
        k = k_ref[0, pl.ds(_chunk_start(c, tkc), tkc), h * LANES:(h + 1) * LANES]
        return _dot_nt(qs[h], k), None

    def pv_rhs(c, h):
        return _pair_pv_rhs(v_ref[0, pl.ds(_chunk_start(c, tkc), tkc), :], h)

    _flash_loop(n_chunks=k_ref.shape[1] // tkc, n_streams=2, scores=scores, pv_rhs=pv_rhs,
                m_ref=m_ref, acc_ref=acc_ref, p_refs=(p0_ref, p1_ref), a_refs=(a0_ref, a1_ref))
    o_ref[0] = (acc_ref[:, :LANES] / acc_ref[:, LANES:]).astype(BF16)


def _mla_flash(q, k, v, tq=1024, tkc=512):
    b, s, nv = v.shape
    pairs = nv // LANES
    tkc = min(tkc, s // 2)
    return pl.pallas_call(
        functools.partial(_mla_flash_kernel, tkc=tkc),
        grid=(b, pairs, s // tq),
        in_specs=[pl.BlockSpec((1, tq, 2 * LANES), lambda i, p, a: (i, a, p)),
                  pl.BlockSpec((1, s, 2 * LANES), lambda i, p, a: (i, 0, p)),
                  pl.BlockSpec((1, s, LANES), lambda i, p, a: (i, 0, p))],
        out_specs=pl.BlockSpec((1, tq, LANES), lambda i, p, a: (i, a, p)),
        out_shape=jax.ShapeDtypeStruct((b, s, nv), BF16),
        scratch_shapes=_flash_scratch(2, tq, tkc),
        compiler_params=_cparams(("parallel", "parallel", "parallel")),
        name="mla_flash",
    )(q, k, v)


ALIBI_PIECES = 3
POS_SPLIT = 32


def _diff_flash_kernel(q_ref, k_ref, v_ref, kpos_ref, lam_ref, subg_ref, o_ref, m_ref, acc_ref,
                       p0_ref, p1_ref, a0_ref, a1_ref, qx_ref, row_ref, *, tq, lambda_init):
    h = pl.program_id(1)
    d = pl.program_id(2)
    n = k_ref.shape[1] // tq
    q = q_ref[0]
    qs = jnp.concatenate([_mask_half(q, 0), _mask_half(q, 1)], axis=0)
    slope = jnp.exp2(-(8.0 / DIFF_HEADS) * (h + 1).astype(F32) * jnp.ones((1, 1), F32)) * LOG2E
    pieces = []
    rest = slope
    for _ in range(ALIBI_PIECES):
        piece = rest.astype(BF16).astype(F32)
        pieces.append(piece)
        rest = rest - piece
    lane = lax.broadcasted_iota(jnp.int32, (1, LANES), 1)
    aug = jnp.zeros((1, LANES), F32)
    for j, piece in enumerate(pieces):
        aug = jnp.where((lane == j) | (lane == j + ALIBI_PIECES), piece, aug)
    for side, sign in enumerate((1.0, -1.0)):
        aug_rows = jnp.broadcast_to(sign * aug, (2 * tq, LANES)).astype(BF16)
        qx_ref[side] = jnp.concatenate([qs, aug_rows], axis=1)
    r = lax.broadcasted_iota(jnp.int32, (2 * tq, LANES), 0) & (tq - 1)
    row_ref[...] = r.astype(F32) * slope

    def chunk_of(i):
        return d if isinstance(i, int) and i == 0 else jnp.remainder(d + i, n)

    def scores(i, _):
        c = chunk_of(i)
        k = k_ref[0, pl.ds(pl.multiple_of(c * tq, tq), tq), :]
        if isinstance(i, int) and i == 0:
            rel = (lax.broadcasted_iota(jnp.int32, (tq, tq), 0)
                   - lax.broadcasted_iota(jnp.int32, (tq, tq), 1))
            bias = jnp.abs(rel).astype(F32) * slope
            return _dot_nt(qs, k) - jnp.concatenate([bias, bias], axis=0), None
        before = c < d
        s = _dot_nt(qx_ref[jnp.where(before, 0, 1)], jnp.concatenate([k, kpos_ref[...]], axis=1))
        far = (jnp.abs(d - c) * tq).astype(F32)
        shift = -(slope * far) - jnp.where(before, 1.0, -1.0) * row_ref[...]
        return s, shift

    def pv_rhs(i, _):
        v = v_ref[0, pl.ds(pl.multiple_of(chunk_of(i) * tq, tq), tq), :]
        return jnp.concatenate([v, jnp.ones(v.shape, BF16)], axis=1)

    _flash_loop(n_chunks=n, n_streams=1, scores=scores, pv_rhs=pv_rhs,
                m_ref=m_ref, acc_ref=acc_ref, p_refs=(p0_ref, p1_ref), a_refs=(a0_ref, a1_ref))

    lp = lam_ref[...]
    lam = (jnp.exp(jnp.sum(lp[0:1] * lp[1:2], axis=1, keepdims=True))
           - jnp.exp(jnp.sum(lp[2:3] * lp[3:4], axis=1, keepdims=True)) + lambda_init)
    o = acc_ref[:, :LANES] / acc_ref[:, LANES:]
    o = o[:tq] - lam * o[tq:]
    o = o * lax.rsqrt(jnp.mean(o * o, axis=-1, keepdims=True) + EPS) * subg_ref[...]
    o_ref[0] = (o * (1.0 - lambda_init)).astype(BF16)


def _diff_kpos_table(tq):
    c = np.arange(tq)
    tab = np.zeros((tq, LANES), np.float32)
    tab[:, :ALIBI_PIECES] = (POS_SPLIT * (c // POS_SPLIT))[:, None]
    tab[:, ALIBI_PIECES:2 * ALIBI_PIECES] = (c % POS_SPLIT)[:, None]
    return jnp.asarray(tab, BF16)


def _diff_flash(q, k, v, lam_p, sub_g, lambda_init, tq=512):
    b, s, n = v.shape
    heads = n // LANES
    tq = min(tq, s // 2)
    assert tq & (tq - 1) == 0 and tq <= POS_SPLIT * 256
    return pl.pallas_call(
        functools.partial(_diff_flash_kernel, tq=tq, lambda_init=lambda_init),
        grid=(b, heads, s // tq),
        in_specs=[pl.BlockSpec((1, tq, LANES), lambda i, p, a: (i, a, p)),
                  pl.BlockSpec((1, s, LANES), lambda i, p, a: (i, 0, p)),
                  pl.BlockSpec((1, s, LANES), lambda i, p, a: (i, 0, p)),
                  _const_spec((tq, LANES)),
                  _const_spec(lam_p.shape),
                  _const_spec(sub_g.shape)],
        out_specs=pl.BlockSpec((1, tq, LANES), lambda i, p, a: (i, a, p)),
        out_shape=jax.ShapeDtypeStruct((b, s, n), BF16),
        scratch_shapes=_flash_scratch(1, 2 * tq, tq) + [pltpu.VMEM((2, 2 * tq, 2 * LANES), BF16),
                                                        pltpu.VMEM((2 * tq, LANES), F32)],
        compiler_params=_cparams(("parallel", "parallel", "parallel")),
        name="diff_flash",
    )(q, k, v, _diff_kpos_table(tq), lam_p, sub_g)


NA_ROWS_PER_STEP = 8
NA_BLOCK = NA_ROWS_PER_STEP * GRID_W
NA_WIN = NA_KH * GRID_W


def _na_kernel(q_ref, kp_ref, kc_ref, kn_ref, vp_ref, vc_ref, vn_ref, bias_ref, o_ref,
               kw_ref, vw_ref, *, rows):
    blk = pl.program_id(2)
    kw_ref[0:NA_BLOCK] = kp_ref[0]
    kw_ref[NA_BLOCK:2 * NA_BLOCK] = kc_ref[0]
    kw_ref[2 * NA_BLOCK:] = kn_ref[0]
    vw_ref[0:NA_BLOCK] = vp_ref[0]
    vw_ref[NA_BLOCK:2 * NA_BLOCK] = vc_ref[0]
    vw_ref[2 * NA_BLOCK:] = vn_ref[0]
    lo = _lo_mask()
    for rl in range(NA_ROWS_PER_STEP):
        r = blk * NA_ROWS_PER_STEP + rl
        r0 = jnp.clip(r - NA_KH // 2, 0, rows - NA_KH)
        start = pl.multiple_of((r0 - blk * NA_ROWS_PER_STEP + NA_ROWS_PER_STEP) * GRID_W, GRID_W)
        off = r0 - r + (NA_KH - 1)
        kwin = kw_ref[pl.ds(start, NA_WIN), :]
        vwin = vw_ref[pl.ds(start, NA_WIN), :]
        qr = q_ref[0, rl * GRID_W:(rl + 1) * GRID_W, :]
        qs = jnp.concatenate([_mask_half(qr, 0), _mask_half(qr, 1)], axis=0)
        bias = jnp.concatenate(
            [jnp.concatenate([bias_ref[half, off + 2 * j] for j in range(NA_KH // 2)], axis=1)
             for half in range(2)], axis=0)
        s = _dot_nt(qs, kwin) + bias
        p = jnp.exp(s - jnp.max(s, axis=1, keepdims=True))
        o = _dot(p.astype(BF16), vwin) / jnp.sum(p, axis=1, keepdims=True)
        o_ref[0, rl * GRID_W:(rl + 1) * GRID_W, :] = jnp.where(lo, o[:GRID_W], o[GRID_W:]).astype(BF16)


def _na_attention(q, k, v, bias_tab):
    b, s, n = q.shape
    pairs = n // LANES
    rows = s // GRID_W
    nblk = s // NA_BLOCK
    blk_spec = lambda f: pl.BlockSpec((1, NA_BLOCK, LANES), f)
    prev = lambda i, p, a: (i, jnp.maximum(a - 1, 0), p)
    cur = lambda i, p, a: (i, a, p)
    nxt = lambda i, p, a: (i, jnp.minimum(a + 1, nblk - 1), p)
    return pl.pallas_call(
        functools.partial(_na_kernel, rows=rows),
        grid=(b, pairs, nblk),
        in_specs=[blk_spec(cur), blk_spec(prev), blk_spec(cur), blk_spec(nxt),
                  blk_spec(prev), blk_spec(cur), blk_spec(nxt),
                  pl.BlockSpec((2,) + bias_tab.shape[1:], lambda i, p, a: (p, 0, 0, 0))],
        out_specs=blk_spec(cur),
        out_shape=jax.ShapeDtypeStruct((b, s, n), BF16),
        scratch_shapes=[pltpu.VMEM((3 * NA_BLOCK, LANES), BF16),
                        pltpu.VMEM((3 * NA_BLOCK, LANES), BF16)],
        compiler_params=_cparams(("parallel", "parallel", "parallel")),
        name="na_attention",
    )(q, k, k, k, v, v, v, bias_tab)


def _na_bias_table(rpb):
    col = np.arange(GRID_W)
    col_start = np.clip(col - NA_KW // 2, 0, GRID_W - NA_KW)
    valid = (col[None, :] >= col_start[:, None]) & (col[None, :] < col_start[:, None] + NA_KW)
    dc = np.clip(col[None, :] - col[:, None] + NA_KW - 1, 0, 2 * NA_KW - 2)
    t = rpb.astype(F32)[:, :, dc]
    t = jnp.where(jnp.asarray(valid)[None, None], t, NEG_INF)
    return jnp.concatenate([t[:, :-1], t[:, 1:]], axis=-1)


FF_CHUNK = 256


def _post_kernel(x_ref, a_ref, mod_ref, wo_ref, ng_ref, w1_ref, w2_ref, o_ref):
    x = x_ref[0]
    x = x + mod_ref[0, 2:3, :] * _dot(a_ref[0], wo_ref[...])
    h = _norm_mod(x, ng_ref[...], mod_ref[0, 3:4, :], mod_ref[0, 4:5, :]).astype(BF16)
    y = None
    for c in range(D_FF // FF_CHUNK):
        g = _dot(h, w1_ref[:, c * FF_CHUNK:(c + 1) * FF_CHUNK])
        u = _dot(h, w1_ref[:, D_FF + c * FF_CHUNK:D_FF + (c + 1) * FF_CHUNK])
        act = (g * (1.0 / (1.0 + jnp.exp(-g))) * u).astype(BF16)
        part = _dot(act, w2_ref[c * FF_CHUNK:(c + 1) * FF_CHUNK, :])
        y = part if y is None else y + part
    o_ref[0] = x + mod_ref[0, 5:6, :] * y


def _post(x, attn, mod, wo, ng, w1, w2, tm=512):
    b, s, d = x.shape
    single = pl.Buffered(1)
    return pl.pallas_call(
        _post_kernel,
        grid=(b, s // tm),
        in_specs=[pl.BlockSpec((1, tm, d), lambda i, j: (i, j, 0)),
                  pl.BlockSpec((1, tm, attn.shape[2]), lambda i, j: (i, j, 0)),
                  pl.BlockSpec((1, 6, d), lambda i, j: (i, 0, 0)),
                  pl.BlockSpec(wo.shape, lambda i, j: (0, 0), pipeline_mode=single),
                  _const_spec((1, d)),
                  pl.BlockSpec(w1.shape, lambda i, j: (0, 0), pipeline_mode=single),
                  pl.BlockSpec(w2.shape, lambda i, j: (0, 0), pipeline_mode=single)],
        out_specs=pl.BlockSpec((1, tm, d), lambda i, j: (i, j, 0)),
        out_shape=jax.ShapeDtypeStruct((b, s, d), F32),
        compiler_params=_cparams(("parallel", "parallel")),
        name="post_ffn",
    )(x, attn, mod, wo, ng, w1, w2)


def _block_diag_ones(block):
    idx = np.arange(MXU_N) // block
    return jnp.asarray(idx[:, None] == idx[None, :], dtype=BF16)


def _rope_angles(pos, dim):
    inv = ROPE_THETA ** (-jnp.arange(0, dim, 2, dtype=F32) / dim)
    ang = pos.astype(F32)[:, None] * inv[None, :]
    return jnp.concatenate([ang, ang], axis=-1)


def _sin_signed(ang):
    half = ang.shape[-1] // 2
    sign = jnp.concatenate([-jnp.ones((half,), F32), jnp.ones((half,), F32)])
    return jnp.sin(ang) * sign


def _axial_tables(s):
    t = jnp.arange(s)
    half = GQA_HEAD_DIM // 2
    ar, ac = _rope_angles(t // GRID_W, half), _rope_angles(t % GRID_W, half)
    cos = jnp.concatenate([jnp.cos(ar), jnp.cos(ac)], axis=-1)
    sin = jnp.concatenate([_sin_signed(ar), _sin_signed(ac)], axis=-1)
    return jnp.tile(cos, (1, 2)), jnp.tile(sin, (1, 2))


def _mla_tables(s):
    ang = _rope_angles(jnp.arange(s), MLA_ROPE)
    ones = jnp.ones((s, MLA_NOPE), F32)
    zpad = jnp.zeros((s, MLA_SLOT - MLA_NOPE - MLA_ROPE), F32)
    cos = jnp.concatenate([ones, jnp.cos(ang), zpad], axis=-1)
    sin = jnp.concatenate([jnp.zeros_like(ones), _sin_signed(ang), zpad], axis=-1)
    return cos, sin


def _gqa_q_order():
    group = GQA_Q_HEADS // GQA_KV_HEADS
    order = []
    for p in range(GQA_KV_HEADS // 2):
        for g in range(group):
            order += [group * (2 * p) + g, group * (2 * p + 1) + g]
    return np.asarray(order)


def _head_cols(order, width):
    return (np.asarray(order)[:, None] * width + np.arange(width)[None, :]).reshape(-1)


def _mla_weights(w_in, w_q_up, w_kv_up):
    d = w_in.shape[0]
    win = jnp.concatenate([w_in, jnp.zeros((d, MLA_LAT_PAD - w_in.shape[1]), w_in.dtype)], axis=1)
    qd = MLA_NOPE + MLA_ROPE
    wq = w_q_up.reshape(MLA_Q_LORA, MLA_HEADS, qd)
    wq = jnp.concatenate([wq, jnp.zeros((MLA_Q_LORA, MLA_HEADS, MLA_SLOT - qd), wq.dtype)], axis=-1)
    wq = wq.reshape(MLA_Q_LORA, MLA_HEADS * MLA_SLOT)
    wkv = w_kv_up.reshape(MLA_KV_LORA, MLA_HEADS, MLA_NOPE + MLA_V)
    k_nope = jnp.concatenate(
        [wkv[..., :MLA_NOPE], jnp.zeros((MLA_KV_LORA, MLA_HEADS, MLA_SLOT - MLA_NOPE), wkv.dtype)], axis=-1)
    place = np.zeros((LANES, MLA_HEADS, MLA_SLOT), np.float32)
    for i in range(MLA_ROPE):
        place[i, :, MLA_NOPE + i] = 1.0
    wk = jnp.concatenate([k_nope.reshape(MLA_KV_LORA, -1),
                          jnp.asarray(place.reshape(LANES, -1), wkv.dtype)], axis=0)
    wv = wkv[..., MLA_NOPE:].reshape(MLA_KV_LORA, MLA_HEADS * MLA_V)
    return win.astype(BF16), wq.astype(BF16), wk.astype(BF16), wv.astype(BF16)


def _slot_gain(g):
    g = jnp.concatenate([g, jnp.zeros((MLA_SLOT - g.shape[0],), g.dtype)])
    return jnp.tile(g, MXU_N // MLA_SLOT)[None, :]


def _trunk(x, mods, p):
    s = x.shape[1]
    for l in range(DEPTH):
        mod = mods[l]
        ng1 = p['norm_g'][l, 0][None, :]
        ng2 = p['norm_g'][l, 1][None, :]
        kind, j = l % 4, l // 4
        if kind == 0:
            n = NA_HEADS * NA_HEAD_DIM
            q, k, v = _qkv_proj(x, mod, ng1, p['na_w_in'][j],
                                jnp.tile(p['na_qk_g'][j, 0], NA_HEADS)[None, :],
                                jnp.tile(p['na_qk_g'][j, 1], NA_HEADS)[None, :],
                                p['gm64'], n, n, n, NA_HEAD_DIM ** -0.5)
            a = _na_attention(q, k, v, p['na_bias'][j])
            wo = p['na_w_out'][j]
        elif kind == 1:
            win, wq, wk, wv = p['mla_w'][j]
            cos, sin = p['mla_tabs'][s]
            q, k, v = _mla_proj(x, mod, ng1, win, p['mla_q_lat_g'][j][None, :],
                                p['mla_kv_lat_g'][j][None, :], wq, wk, wv,
                                _slot_gain(p['mla_qk_g'][j, 0]), _slot_gain(p['mla_qk_g'][j, 1]),
                                p['gm128'], cos, sin, (MLA_NOPE + MLA_ROPE) ** -0.5 * LOG2E)
            a = _mla_flash(q, k, v)
            wo = p['mla_w_out'][j]
        elif kind == 2:
            n = DIFF_HEADS * 2 * DIFF_HEAD_DIM
            lambda_init = 0.8 - 0.6 * math.exp(-0.3 * l)
            q, k, v = _qkv_proj(x, mod, ng1, p['diff_w_in'][j],
                                jnp.tile(p['diff_qk_g'][j, 0], 2 * DIFF_HEADS)[None, :],
                                jnp.tile(p['diff_qk_g'][j, 1], 2 * DIFF_HEADS)[None, :],
                                p['gm64'], n, n, n, DIFF_HEAD_DIM ** -0.5 * LOG2E)
            a = _diff_flash(q, k, v, p['diff_lambda'][j], p['diff_sub_g'][j][None, :], lambda_init)
            wo = p['diff_w_out'][j]
        else:
            nq = GQA_Q_HEADS * GQA_HEAD_DIM
            nkv = GQA_KV_HEADS * GQA_HEAD_DIM
            q, k, v = _qkv_proj(x, mod, ng1, p['gqa_w_in'][j],
                                jnp.tile(p['gqa_qk_g'][j, 0], GQA_Q_HEADS)[None, :],
                                jnp.tile(p['gqa_qk_g'][j, 1], GQA_KV_HEADS)[None, :],
                                p['gm64'], nq, nkv, nkv, GQA_HEAD_DIM ** -0.5 * LOG2E,
                                rope_tabs=p['gqa_tabs'][s])
            a = _gqa_flash(q, k, v)
            wo = p['gqa_w_out'][j]
        x = _post(x, a, mod, wo, ng2, p['ffn_w_in'][l], p['ffn_w_out'][l])
    return x


def kernel(x_prompt, x_sample, c_prompt, c_sample, norm_g, ada_w, ada_b, na_w_in, na_qk_g, na_rpb, na_w_out, mla_w_in, mla_q_lat_g, mla_kv_lat_g, mla_w_q_up, mla_w_kv_up, mla_qk_g, mla_w_out, diff_w_in, diff_qk_g, diff_lambda, diff_sub_g, diff_w_out, gqa_w_in, gqa_qk_g, gqa_w_out, ffn_w_in, ffn_w_out):
    bp, bs = x_prompt.shape[0], x_sample.shape[0]
    d = x_prompt.shape[2]
    rows = -(-(bp + bs) // 8) * 8
    c_all = jnp.concatenate([c_prompt, c_sample, jnp.zeros((rows - bp - bs, d), F32)], axis=0)
    mods = _ada_mod(c_all, ada_w, ada_b)
    mods_p = mods[:, :bp].reshape(DEPTH, bp, 6, d)
    mods_s = mods[:, bp:bp + bs].reshape(DEPTH, bs, 6, d)

    q_cols = _head_cols(_gqa_q_order(), GQA_HEAD_DIM)
    nq = GQA_Q_HEADS * GQA_HEAD_DIM
    gqa_in = jnp.concatenate([gqa_w_in[:, :, :nq][:, :, q_cols], gqa_w_in[:, :, nq:]], axis=2)
    seqs = sorted({x_prompt.shape[1], x_sample.shape[1]})
    p = {
        'norm_g': norm_g,
        'gm64': _block_diag_ones(64), 'gm128': _block_diag_ones(128),
        'na_w_in': na_w_in.astype(BF16), 'na_qk_g': na_qk_g, 'na_w_out': na_w_out.astype(BF16),
        'na_bias': [_na_bias_table(na_rpb[j]) for j in range(na_rpb.shape[0])],
        'mla_w': [_mla_weights(mla_w_in[j], mla_w_q_up[j], mla_w_kv_up[j])
                  for j in range(mla_w_in.shape[0])],
        'mla_tabs': {s: _mla_tables(s) for s in seqs},
        'mla_q_lat_g': mla_q_lat_g, 'mla_kv_lat_g': mla_kv_lat_g, 'mla_qk_g': mla_qk_g,
        'mla_w_out': mla_w_out.astype(BF16),
        'diff_w_in': diff_w_in.astype(BF16), 'diff_qk_g': diff_qk_g, 'diff_lambda': diff_lambda,
        'diff_sub_g': diff_sub_g, 'diff_w_out': diff_w_out.astype(BF16),
        'gqa_w_in': gqa_in.astype(BF16), 'gqa_qk_g': gqa_qk_g,
        'gqa_w_out': gqa_w_out[:, q_cols, :].astype(BF16),
        'gqa_tabs': {s: _axial_tables(s) for s in seqs},
        'ffn_w_in': ffn_w_in.astype(BF16), 'ffn_w_out': ffn_w_out.astype(BF16),
    }
    return (_trunk(x_prompt, mods_p, p), _trunk(x_sample, mods_s, p))
```

```python
import functools
import math

import numpy as np
import jax
import jax.numpy as jnp
from jax import lax
from jax.experimental import pallas as pl
from jax.experimental.pallas import tpu as pltpu

D_MODEL = 1024
DEPTH = 4
GRID_W = 64
EPS = 1e-6
NEG_INF = -1e30
ROPE_THETA = 10000.0
NA_HEADS, NA_HEAD_DIM, NA_KH, NA_KW = 16, 64, 8, 16
MLA_HEADS, MLA_Q_LORA, MLA_KV_LORA, MLA_NOPE, MLA_ROPE, MLA_V = 16, 384, 256, 64, 32, 64
DIFF_HEADS, DIFF_HEAD_DIM = 8, 64
GQA_Q_HEADS, GQA_KV_HEADS, GQA_HEAD_DIM = 16, 4, 64
D_FF = -(-8 * D_MODEL // (3 * 256)) * 256

LANES = 128
MXU_N = 256
ROT_HALF = 16
assert MLA_ROPE == 2 * ROT_HALF and GQA_HEAD_DIM // 2 == 2 * ROT_HALF
LOG2E = math.log2(math.e)
VMEM_LIMIT = 56 * 1024 * 1024

F32 = jnp.float32
BF16 = jnp.bfloat16


def _cparams(sem):
    return pltpu.CompilerParams(dimension_semantics=sem, vmem_limit_bytes=VMEM_LIMIT)


def _const_spec(shape):
    nd = len(shape)
    return pl.BlockSpec(shape, lambda *_: (0,) * nd)


def _dot(a, b):
    return jnp.dot(a, b, preferred_element_type=F32)


def _dot_nt(a, b):
    return lax.dot_general(a, b, (((1,), (1,)), ((), ())), preferred_element_type=F32)


def _ada_kernel(c_ref, w_ref, b_ref, o_ref):
    c = c_ref[...]
    ca = c * (1.0 / (1.0 + jnp.exp(-c)))
    o_ref[0] = jnp.dot(ca, w_ref[0], preferred_element_type=F32,
                       precision=lax.Precision.HIGHEST) + b_ref[0]


def _ada_mod(c_all, ada_w, ada_b):
    rows = c_all.shape[0]
    depth, d, n = ada_w.shape
    tn = 1536
    return pl.pallas_call(
        _ada_kernel,
        grid=(depth, n // tn),
        in_specs=[pl.BlockSpec((rows, d), lambda l, j: (0, 0)),
                  pl.BlockSpec((1, d, tn), lambda l, j: (l, 0, j)),
                  pl.BlockSpec((1, 1, tn), lambda l, j: (l, 0, j))],
        out_specs=pl.BlockSpec((1, rows, tn), lambda l, j: (l, 0, j)),
        out_shape=jax.ShapeDtypeStruct((depth, rows, n), F32),
        compiler_params=_cparams(("arbitrary", "arbitrary")),
        name="ada_mod",
    )(c_all, ada_w, ada_b.reshape(depth, 1, n))


def _norm_mod(x, g, shift, scale):
    ms = jnp.mean(x * x, axis=-1, keepdims=True)
    y = x * lax.rsqrt(ms + EPS)
    return (y * g) * (1.0 + scale) + shift


def _group_rms(a, gmat, inv_n):
    ssq = _dot((a * a).astype(BF16), gmat)
    return a * lax.rsqrt(ssq * inv_n + EPS)


def _rope_chunk(a, cos, sin_signed, first_half):
    fwd = pltpu.roll(a, ROT_HALF, 1)
    bwd = pltpu.roll(a, LANES - ROT_HALF, 1)
    return a * cos + jnp.where(first_half, bwd, fwd) * sin_signed


def _first_half_mask():
    lane = lax.broadcasted_iota(jnp.int32, (1, LANES), 1)
    return (lane & ROT_HALF) == 0


def _qkv_kernel(*refs, n_q, n_k, n_v, q_const, rope):
    if rope:
        (x_ref, mod_ref, ng_ref, w_ref, gq_ref, gk_ref, gm_ref, cos_ref, sin_ref,
         q_ref, k_ref, v_ref) = refs
    else:
        (x_ref, mod_ref, ng_ref, w_ref, gq_ref, gk_ref, gm_ref,
         q_ref, k_ref, v_ref) = refs
    h = _norm_mod(x_ref[0], ng_ref[...], mod_ref[0, 0:1, :], mod_ref[0, 1:2, :])
    hb = h.astype(BF16)
    gmat = gm_ref[...]
    if rope:
        cos = cos_ref[...]
        sin = sin_ref[...]
        fh = _first_half_mask()

    def normed(col0, gain, const):
        a = _dot(hb, w_ref[:, col0:col0 + MXU_N])
        a = _group_rms(a, gmat, 1.0 / 64.0) * (gain * const)
        if rope:
            a = jnp.concatenate(
                [_rope_chunk(a[:, j * LANES:(j + 1) * LANES], cos, sin, fh)
                 for j in range(MXU_N // LANES)], axis=1)
        return a.astype(BF16)

    for c in range(n_q // MXU_N):
        sl = slice(c * MXU_N, (c + 1) * MXU_N)
        q_ref[0, :, sl] = normed(c * MXU_N, gq_ref[:, sl], q_const)
    for c in range(n_k // MXU_N):
        sl = slice(c * MXU_N, (c + 1) * MXU_N)
        k_ref[0, :, sl] = normed(n_q + c * MXU_N, gk_ref[:, sl], 1.0)
    for c in range(n_v // MXU_N):
        sl = slice(c * MXU_N, (c + 1) * MXU_N)
        v_ref[0, :, sl] = _dot(hb, w_ref[:, n_q + n_k + c * MXU_N:
                                          n_q + n_k + (c + 1) * MXU_N]).astype(BF16)


def _qkv_proj(x, mod, ng, w, gq, gk, gmat, n_q, n_k, n_v, q_const, rope_tabs=None, tm=512):
    b, s, d = x.shape
    n = n_q + n_k + n_v
    rope = rope_tabs is not None
    in_specs = [pl.BlockSpec((1, tm, d), lambda i, j: (i, j, 0)),
                pl.BlockSpec((1, 6, d), lambda i, j: (i, 0, 0)),
                _const_spec((1, d)),
                _const_spec((d, n)),
                _const_spec((1, n_q)),
                _const_spec((1, n_k)),
                _const_spec((MXU_N, MXU_N))]
    args = [x, mod, ng, w, gq, gk, gmat]
    if rope:
        in_specs += [pl.BlockSpec((tm, LANES), lambda i, j: (j, 0))] * 2
        args += list(rope_tabs)
    outs = [jax.ShapeDtypeStruct((b, s, m), BF16) for m in (n_q, n_k, n_v)]
    out_specs = [pl.BlockSpec((1, tm, m), lambda i, j: (i, j, 0)) for m in (n_q, n_k, n_v)]
    return pl.pallas_call(
        functools.partial(_qkv_kernel, n_q=n_q, n_k=n_k, n_v=n_v, q_const=q_const, rope=rope),
        grid=(b, s // tm),
        in_specs=in_specs,
        out_specs=out_specs,
        out_shape=outs,
        compiler_params=_cparams(("parallel", "parallel")),
        name="qkv_proj",
    )(*args)


MLA_LAT_PAD = 768
MLA_SLOT = 128


def _mla_proj_kernel(x_ref, mod_ref, ng_ref, win_ref, gql_ref, gkvl_ref, wq_ref, wk_ref, wv_ref,
                     gq_ref, gk_ref, gm_ref, cos_ref, sin_ref, q_ref, k_ref, v_ref, *, q_const):
    h = _norm_mod(x_ref[0], ng_ref[...], mod_ref[0, 0:1, :], mod_ref[0, 1:2, :])
    lat = _dot(h.astype(BF16), win_ref[...])
    q_lat = lat[:, :MLA_Q_LORA]
    kv_lat = lat[:, MLA_Q_LORA:MLA_Q_LORA + MLA_KV_LORA]
    k_rope = lat[:, MLA_Q_LORA + MLA_KV_LORA:]
    qn = q_lat * lax.rsqrt(jnp.mean(q_lat * q_lat, axis=-1, keepdims=True) + EPS) * gql_ref[...]
    kvn = kv_lat * lax.rsqrt(jnp.mean(kv_lat * kv_lat, axis=-1, keepdims=True) + EPS) * gkvl_ref[...]
    qnb = qn.astype(BF16)
    kvnb = kvn.astype(BF16)
    kvr = jnp.concatenate([kvnb, k_rope.astype(BF16)], axis=1)
    gmat = gm_ref[...]
    cos = cos_ref[...]
    sin = sin_ref[...]
    fh = _first_half_mask()
    inv_n = 1.0 / (MLA_NOPE + MLA_ROPE)
    n_slots = MLA_HEADS * MLA_SLOT

    def finish(a, gain, const):
        a = _group_rms(a, gmat, inv_n) * (gain * const)
        a = jnp.concatenate(
            [_rope_chunk(a[:, j * LANES:(j + 1) * LANES], cos, sin, fh)
             for j in range(MXU_N // LANES)], axis=1)
        return a.astype(BF16)

    for c in range(n_slots // MXU_N):
        sl = slice(c * MXU_N, (c + 1) * MXU_N)
        q_ref[0, :, sl] = finish(_dot(qnb, wq_ref[:, sl]), gq_ref[...], q_const)
        k_ref[0, :, sl] = finish(_dot(kvr, wk_ref[:, sl]), gk_ref[...], 1.0)
    for c in range(MLA_HEADS * MLA_V // MXU_N):
        sl = slice(c * MXU_N, (c + 1) * MXU_N)
        v_ref[0, :, sl] = _dot(kvnb, wv_ref[:, sl]).astype(BF16)


def _mla_proj(x, mod, ng, win, gql, gkvl, wq, wk, wv, gq, gk, gmat, cos, sin, q_const, tm=512):
    b, s, d = x.shape
    n_slots = MLA_HEADS * MLA_SLOT
    n_v = MLA_HEADS * MLA_V
    in_specs = [pl.BlockSpec((1, tm, d), lambda i, j: (i, j, 0)),
                pl.BlockSpec((1, 6, d), lambda i, j: (i, 0, 0)),
                _const_spec((1, d)),
                _const_spec(win.shape),
                _const_spec(gql.shape),
                _const_spec(gkvl.shape),
                _const_spec(wq.shape),
                _const_spec(wk.shape),
                _const_spec(wv.shape),
                _const_spec(gq.shape),
                _const_spec(gk.shape),
                _const_spec((MXU_N, MXU_N)),
                pl.BlockSpec((tm, LANES), lambda i, j: (j, 0)),
                pl.BlockSpec((tm, LANES), lambda i, j: (j, 0))]
    outs = [jax.ShapeDtypeStruct((b, s, m), BF16) for m in (n_slots, n_slots, n_v)]
    out_specs = [pl.BlockSpec((1, tm, m), lambda i, j: (i, j, 0)) for m in (n_slots, n_slots, n_v)]
    return pl.pallas_call(
        functools.partial(_mla_proj_kernel, q_const=q_const),
        grid=(b, s // tm),
        in_specs=in_specs,
        out_specs=out_specs,
        out_shape=outs,
        compiler_params=_cparams(("parallel", "parallel")),
        name="mla_proj",
    )(x, mod, ng, win, gql, gkvl, wq, wk, wv, gq, gk, gmat, cos, sin)


HEAD_LANES = 64


def _lo_mask():
    return lax.broadcasted_iota(jnp.int32, (1, LANES), 1) < HEAD_LANES


def _chunk_start(c, tkc):
    return c * tkc if isinstance(c, int) else pl.multiple_of(c * tkc, tkc)


def _flash_loop(*, n_chunks, n_streams, scores, pv_rhs, m_ref, acc_ref, p_refs, a_refs):
    lo = _lo_mask()
    m_ref[...] = jnp.full(m_ref.shape, NEG_INF, F32)
    acc_ref[...] = jnp.zeros(acc_ref.shape, F32)

    def softmax_stage(i, slot):
        for h in range(n_streams):
            s, shift = scores(i, h)
            m_prev = m_ref[h]
            row_max = jnp.max(s, axis=1, keepdims=True)
            if shift is not None:
                row_max = row_max + shift
            m_next = jnp.maximum(m_prev, row_max)
            a_refs[slot][h] = jnp.exp2(m_prev - m_next)
            m_ref[h] = m_next
            m_eff = m_next if shift is None else m_next - shift
            m_rep = jnp.concatenate([m_eff] * (s.shape[1] // LANES), axis=1)
            p_refs[slot][h] = jnp.exp2(s - m_rep).astype(BF16)

    def pv_stage(i, slot):
        pv = None
        for h in range(n_streams):
            t = _dot(p_refs[slot][h], pv_rhs(i, h))
            pv = t if pv is None else pv + t
        a = a_refs[slot][0]
        if n_streams == 2:
            a = jnp.where(lo, a, a_refs[slot][1])
        acc_ref[...] = acc_ref[...] * jnp.concatenate([a, a], axis=1) + pv

    def body(j, carry):
        softmax_stage(2 * j, 0)
        pv_stage(2 * j - 1, 1)
        softmax_stage(2 * j + 1, 1)
        pv_stage(2 * j, 0)
        return carry

    assert n_chunks % 2 == 0
    softmax_stage(0, 0)
    softmax_stage(1, 1)
    pv_stage(0, 0)
    lax.fori_loop(1, n_chunks // 2, body, 0, unroll=4)
    pv_stage(n_chunks - 1, 1)


def _pair_pv_rhs(v, h):
    msk = _half_mask(v.shape, h)
    rhs = jnp.concatenate([jnp.where(msk, v.astype(F32), 0.0), jnp.where(msk, 1.0, 0.0)], axis=1)
    return rhs.astype(BF16)


def _half_mask(shape, h):
    lane = lax.broadcasted_iota(jnp.int32, shape, len(shape) - 1)
    return (lane < HEAD_LANES) if h == 0 else (lane >= HEAD_LANES)


def _mask_half(x, h):
    return jnp.where(_half_mask(x.shape, h), x.astype(F32), 0.0).astype(BF16)


def _flash_scratch(n_streams, m, tkc):
    return [pltpu.VMEM((n_streams, m, LANES), F32),
            pltpu.VMEM((m, 2 * LANES), F32),
            pltpu.VMEM((n_streams, m, tkc), BF16),
            pltpu.VMEM((n_streams, m, tkc), BF16),
            pltpu.VMEM((n_streams, m, LANES), F32),
            pltpu.VMEM((n_streams, m, LANES), F32)]


def _gqa_flash_kernel(q_ref, k_ref, v_ref, o_ref, m_ref, acc_ref, p0_ref, p1_ref, a0_ref, a1_ref,
                      *, tq, tkc, group):
    q = q_ref[0]
    qs = [jnp.concatenate([_mask_half(q[:, g * LANES:(g + 1) * LANES], h)
                           for g in range(group)], axis=0)
          for h in range(2)]

    def scores(c, h):
        return _dot_nt(qs[h], k_ref[0, pl.ds(_chunk_start(c, tkc), tkc), :]), None

    def pv_rhs(c, h):
        return _pair_pv_rhs(v_ref[0, pl.ds(_chunk_start(c, tkc), tkc), :], h)

    _flash_loop(n_chunks=k_ref.shape[1] // tkc, n_streams=2, scores=scores, pv_rhs=pv_rhs,
                m_ref=m_ref, acc_ref=acc_ref, p_refs=(p0_ref, p1_ref), a_refs=(a0_ref, a1_ref))

    o = acc_ref[:, :LANES] / acc_ref[:, LANES:]
    for g in range(group):
        o_ref[0, :, g * LANES:(g + 1) * LANES] = o[g * tq:(g + 1) * tq].astype(BF16)


def _gqa_flash(q, k, v, tq=256, tkc=512):
    b, s, nq = q.shape
    pairs = k.shape[2] // LANES
    group = nq // k.shape[2]
    wq = group * LANES
    tkc = min(tkc, s // 2)
    return pl.pallas_call(
        functools.partial(_gqa_flash_kernel, tq=tq, tkc=tkc, group=group),
        grid=(b, pairs, s // tq),
        in_specs=[pl.BlockSpec((1, tq, wq), lambda i, p, a: (i, a, p)),
                  pl.BlockSpec((1, s, LANES), lambda i, p, a: (i, 0, p)),
                  pl.BlockSpec((1, s, LANES), lambda i, p, a: (i, 0, p))],
        out_specs=pl.BlockSpec((1, tq, wq), lambda i, p, a: (i, a, p)),
        out_shape=jax.ShapeDtypeStruct((b, s, nq), BF16),
        scratch_shapes=_flash_scratch(2, group * tq, tkc),
        compiler_params=_cparams(("parallel", "parallel", "parallel")),
        name="gqa_flash",
    )(q, k, v)


def _mla_flash_kernel(q_ref, k_ref, v_ref, o_ref, m_ref, acc_ref, p0_ref, p1_ref, a0_ref, a1_ref,
                      *, tkc):
    qs = [q_ref[0, :, :LANES], q_ref[0, :, LANES:]]

    def scores(c, h):
        k = k_ref[0, pl.ds(_chunk_start(c, tkc), tkc), h * LANES:(h + 1) * LANES]
        return _dot_nt(qs[h], k), None

    def pv_rhs(c, h):
        return _pair_pv_rhs(v_ref[0, pl.ds(_chunk_start(c, tkc), tkc), :], h)

    _flash_loop(n_chunks=k_ref.shape[1] // tkc, n_streams=2, scores=scores, pv_rhs=pv_rhs,
                m_ref=m_ref, acc_ref=acc_ref, p_refs=(p0_ref, p1_ref), a_refs=(a0_ref, a1_ref))
    o_ref[0] = (acc_ref[:, :LANES] / acc_ref[:, LANES:]).astype(BF16)


def _mla_flash(q, k, v, tq=1024, tkc=512):
    b, s, nv = v.shape
    pairs = nv // LANES
    tkc = min(tkc, s // 2)
    return pl.pallas_call(
        functools.partial(_mla_flash_kernel, tkc=tkc),
        grid=(b, pairs, s // tq),
        in_specs=[pl.BlockSpec((1, tq, 2 * LANES), lambda i, p, a: (i, a, p)),
                  pl.BlockSpec((1, s, 2 * LANES), lambda i, p, a: (i, 0, p)),
                  pl.BlockSpec((1, s, LANES), lambda i, p, a: (i, 0, p))],
        out_specs=pl.BlockSpec((1, tq, LANES), lambda i, p, a: (i, a, p)),
        out_shape=jax.ShapeDtypeStruct((b, s, nv), BF16),
        scratch_shapes=_flash_scratch(2, tq, tkc),
        compiler_params=_cparams(("parallel", "parallel", "parallel")),
        name="mla_flash",
    )(q, k, v)


ALIBI_PIECES = 3
POS_SPLIT = 32


def _diff_flash_kernel(q_ref, k_ref, v_ref, kpos_ref, lam_ref, subg_ref, o_ref, m_ref, acc_ref,
                       p0_ref, p1_ref, a0_ref, a1_ref, qx_ref, row_ref, *, tq, lambda_init):
    h = pl.program_id(1)
    d = pl.program_id(2)
    n = k_ref.shape[1] // tq
    q = q_ref[0]
    qs = jnp.concatenate([_mask_half(q, 0), _mask_half(q, 1)], axis=0)
    slope = jnp.exp2(-(8.0 / DIFF_HEADS) * (h + 1).astype(F32) * jnp.ones((1, 1), F32)) * LOG2E
    pieces = []
    rest = slope
    for _ in range(ALIBI_PIECES):
        piece = rest.astype(BF16).astype(F32)
        pieces.append(piece)
        rest = rest - piece
    lane = lax.broadcasted_iota(jnp.int32, (1, LANES), 1)
    aug = jnp.zeros((1, LANES), F32)
    for j, piece in enumerate(pieces):
        aug = jnp.where((lane == j) | (lane == j + ALIBI_PIECES), piece, aug)
    for side, sign in enumerate((1.0, -1.0)):
        aug_rows = jnp.broadcast_to(sign * aug, (2 * tq, LANES)).astype(BF16)
        qx_ref[side] = jnp.concatenate([qs, aug_rows], axis=1)
    r = lax.broadcasted_iota(jnp.int32, (2 * tq, LANES), 0) & (tq - 1)
    row_ref[...] = r.astype(F32) * slope

    def chunk_of(i):
        return d if isinstance(i, int) and i == 0 else jnp.remainder(d + i, n)

    def scores(i, _):
        c = chunk_of(i)
        k = k_ref[0, pl.ds(pl.multiple_of(c * tq, tq), tq), :]
        if isinstance(i, int) and i == 0:
            rel = (lax.broadcasted_iota(jnp.int32, (tq, tq), 0)
                   - lax.broadcasted_iota(jnp.int32, (tq, tq), 1))
            bias = jnp.abs(rel).astype(F32) * slope
            return _dot_nt(qs, k) - jnp.concatenate([bias, bias], axis=0), None
        before = c < d
        s = _dot_nt(qx_ref[jnp.where(before, 0, 1)], jnp.concatenate([k, kpos_ref[...]], axis=1))
        far = (jnp.abs(d - c) * tq).astype(F32)
        shift = -(slope * far) - jnp.where(before, 1.0, -1.0) * row_ref[...]
        return s, shift

    def pv_rhs(i, _):
        v = v_ref[0, pl.ds(pl.multiple_of(chunk_of(i) * tq, tq), tq), :]
        return jnp.concatenate([v, jnp.ones(v.shape, BF16)], axis=1)

    _flash_loop(n_chunks=n, n_streams=1, scores=scores, pv_rhs=pv_rhs,
                m_ref=m_ref, acc_ref=acc_ref, p_refs=(p0_ref, p1_ref), a_refs=(a0_ref, a1_ref))

    lp = lam_ref[...]
    lam = (jnp.exp(jnp.sum(lp[0:1] * lp[1:2], axis=1, keepdims=True))
           - jnp.exp(jnp.sum(lp[2:3] * lp[3:4], axis=1, keepdims=True)) + lambda_init)
    o = acc_ref[:, :LANES] / acc_ref[:, LANES:]
    o = o[:tq] - lam * o[tq:]
    o = o * lax.rsqrt(jnp.mean(o * o, axis=-1, keepdims=True) + EPS) * subg_ref[...]
    o_ref[0] = (o * (1.0 - lambda_init)).astype(BF16)


def _diff_kpos_table(tq):
    c = np.arange(tq)
    tab = np.zeros((tq, LANES), np.float32)
    tab[:, :ALIBI_PIECES] = (POS_SPLIT * (c // POS_SPLIT))[:, None]
    tab[:, ALIBI_PIECES:2 * ALIBI_PIECES] = (c % POS_SPLIT)[:, None]
    return jnp.asarray(tab, BF16)


def _diff_flash(q, k, v, lam_p, sub_g, lambda_init, tq=512):
    b, s, n = v.shape
    heads = n // LANES
    tq = min(tq, s // 2)
    assert tq & (tq - 1) == 0 and tq <= POS_SPLIT * 256
    return pl.pallas_call(
        functools.partial(_diff_flash_kernel, tq=tq, lambda_init=lambda_init),
        grid=(b, heads, s // tq),
        in_specs=[pl.BlockSpec((1, tq, LANES), lambda i, p, a: (i, a, p)),
                  pl.BlockSpec((1, s, LANES), lambda i, p, a: (i, 0, p)),
                  pl.BlockSpec((1, s, LANES), lambda i, p, a: (i, 0, p)),
                  _const_spec((tq, LANES)),
                  _const_spec(lam_p.shape),
                  _const_spec(sub_g.shape)],
        out_specs=pl.BlockSpec((1, tq, LANES), lambda i, p, a: (i, a, p)),
        out_shape=jax.ShapeDtypeStruct((b, s, n), BF16),
        scratch_shapes=_flash_scratch(1, 2 * tq, tq) + [pltpu.VMEM((2, 2 * tq, 2 * LANES), BF16),
                                                        pltpu.VMEM((2 * tq, LANES), F32)],
        compiler_params=_cparams(("parallel", "parallel", "parallel")),
        name="diff_flash",
    )(q, k, v, _diff_kpos_table(tq), lam_p, sub_g)


NA_ROWS_PER_STEP = 8
NA_BLOCK = NA_ROWS_PER_STEP * GRID_W
NA_WIN = NA_KH * GRID_W


def _na_kernel(q_ref, kp_ref, kc_ref, kn_ref, vp_ref, vc_ref, vn_ref, bias_ref, o_ref,
               kw_ref, vw_ref, *, rows):
    blk = pl.program_id(2)
    kw_ref[0:NA_BLOCK] = kp_ref[0]
    kw_ref[NA_BLOCK:2 * NA_BLOCK] = kc_ref[0]
    kw_ref[2 * NA_BLOCK:] = kn_ref[0]
    vw_ref[0:NA_BLOCK] = vp_ref[0]
    vw_ref[NA_BLOCK:2 * NA_BLOCK] = vc_ref[0]
    vw_ref[2 * NA_BLOCK:] = vn_ref[0]
    lo = _lo_mask()
    for rl in range(NA_ROWS_PER_STEP):
        r = blk * NA_ROWS_PER_STEP + rl
        r0 = jnp.clip(r - NA_KH // 2, 0, rows - NA_KH)
        start = pl.multiple_of((r0 - blk * NA_ROWS_PER_STEP + NA_ROWS_PER_STEP) * GRID_W, GRID_W)
        off = r0 - r + (NA_KH - 1)
        kwin = kw_ref[pl.ds(start, NA_WIN), :]
        vwin = vw_ref[pl.ds(start, NA_WIN), :]
        qr = q_ref[0, rl * GRID_W:(rl + 1) * GRID_W, :]
        qs = jnp.concatenate([_mask_half(qr, 0), _mask_half(qr, 1)], axis=0)
        bias = jnp.concatenate(
            [jnp.concatenate([bias_ref[half, off + 2 * j] for j in range(NA_KH // 2)], axis=1)
             for half in range(2)], axis=0)
        s = _dot_nt(qs, kwin) + bias
        p = jnp.exp(s - jnp.max(s, axis=1, keepdims=True))
        o = _dot(p.astype(BF16), vwin) / jnp.sum(p, axis=1, keepdims=True)
        o_ref[0, rl * GRID_W:(rl + 1) * GRID_W, :] = jnp.where(lo, o[:GRID_W], o[GRID_W:]).astype(BF16)


def _na_attention(q, k, v, bias_tab):
    b, s, n = q.shape
    pairs = n // LANES
    rows = s // GRID_W
    nblk = s // NA_BLOCK
    blk_spec = lambda f: pl.BlockSpec((1, NA_BLOCK, LANES), f)
    prev = lambda i, p, a: (i, jnp.maximum(a - 1, 0), p)
    cur = lambda i, p, a: (i, a, p)
    nxt = lambda i, p, a: (i, jnp.minimum(a + 1, nblk - 1), p)
    return pl.pallas_call(
        functools.partial(_na_kernel, rows=rows),
        grid=(b, pairs, nblk),
        in_specs=[blk_spec(cur), blk_spec(prev), blk_spec(cur), blk_spec(nxt),
                  blk_spec(prev), blk_spec(cur), blk_spec(nxt),
                  pl.BlockSpec((2,) + bias_tab.shape[1:], lambda i, p, a: (p, 0, 0, 0))],
        out_specs=blk_spec(cur),
        out_shape=jax.ShapeDtypeStruct((b, s, n), BF16),
        scratch_shapes=[pltpu.VMEM((3 * NA_BLOCK, LANES), BF16),
                        pltpu.VMEM((3 * NA_BLOCK, LANES), BF16)],
        compiler_params=_cparams(("parallel", "parallel", "parallel")),
        name="na_attention",
    )(q, k, k, k, v, v, v, bias_tab)


def _na_bias_table(rpb):
    col = np.arange(GRID_W)
    col_start = np.clip(col - NA_KW // 2, 0, GRID_W - NA_KW)
    valid = (col[None, :] >= col_start[:, None]) & (col[None, :] < col_start[:, None] + NA_KW)
    dc = np.clip(col[None, :] - col[:, None] + NA_KW - 1, 0, 2 * NA_KW - 2)
    t = rpb.astype(F32)[:, :, dc]
    t = jnp.where(jnp.asarray(valid)[None, None], t, NEG_INF)
    return jnp.concatenate([t[:, :-1], t[:, 1:]], axis=-1)


FF_CHUNK = 256


def _post_kernel(x_ref, a_ref, mod_ref, wo_ref, ng_ref, w1_ref, w2_ref, o_ref):
    x = x_ref[0]
    x = x + mod_ref[0, 2:3, :] * _dot(a_ref[0], wo_ref[...])
    h = _norm_mod(x, ng_ref[...], mod_ref[0, 3:4, :], mod_ref[0, 4:5, :]).astype(BF16)
    y = None
    for c in range(D_FF // FF_CHUNK):
        g = _dot(h, w1_ref[:, c * FF_CHUNK:(c + 1) * FF_CHUNK])
        u = _dot(h, w1_ref[:, D_FF + c * FF_CHUNK:D_FF + (c + 1) * FF_CHUNK])
        act = (g * (1.0 / (1.0 + jnp.exp(-g))) * u).astype(BF16)
        part = _dot(act, w2_ref[c * FF_CHUNK:(c + 1) * FF_CHUNK, :])
        y = part if y is None else y + part
    o_ref[0] = x + mod_ref[0, 5:6, :] * y


def _post(x, attn, mod, wo, ng, w1, w2, tm=512):
    b, s, d = x.shape
    single = pl.Buffered(1)
    return pl.pallas_call(
        _post_kernel,
        grid=(b, s // tm),
        in_specs=[pl.BlockSpec((1, tm, d), lambda i, j: (i, j, 0)),
                  pl.BlockSpec((1, tm, attn.shape[2]), lambda i, j: (i, j, 0)),
                  pl.BlockSpec((1, 6, d), lambda i, j: (i, 0, 0)),
                  pl.BlockSpec(wo.shape, lambda i, j: (0, 0), pipeline_mode=single),
                  _const_spec((1, d)),
                  pl.BlockSpec(w1.shape, lambda i, j: (0, 0), pipeline_mode=single),
                  pl.BlockSpec(w2.shape, lambda i, j: (0, 0), pipeline_mode=single)],
        out_specs=pl.BlockSpec((1, tm, d), lambda i, j: (i, j, 0)),
        out_shape=jax.ShapeDtypeStruct((b, s, d), F32),
        compiler_params=_cparams(("parallel", "parallel")),
        name="post_ffn",
    )(x, attn, mod, wo, ng, w1, w2)


def _block_diag_ones(block):
    idx = np.arange(MXU_N) // block
    return jnp.asarray(idx[:, None] == idx[None, :], dtype=BF16)


def _rope_angles(pos, dim):
    inv = ROPE_THETA ** (-jnp.arange(0, dim, 2, dtype=F32) / dim)
    ang = pos.astype(F32)[:, None] * inv[None, :]
    return jnp.concatenate([ang, ang], axis=-1)


def _sin_signed(ang):
    half = ang.shape[-1] // 2
    sign = jnp.concatenate([-jnp.ones((half,), F32), jnp.ones((half,), F32)])
    return jnp.sin(ang) * sign


def _axial_tables(s):
    t = jnp.arange(s)
    half = GQA_HEAD_DIM // 2
    ar, ac = _rope_angles(t // GRID_W, half), _rope_angles(t % GRID_W, half)
    cos = jnp.concatenate([jnp.cos(ar), jnp.cos(ac)], axis=-1)
    sin = jnp.concatenate([_sin_signed(ar), _sin_signed(ac)], axis=-1)
    return jnp.tile(cos, (1, 2)), jnp.tile(sin, (1, 2))


def _mla_tables(s):
    ang = _rope_angles(jnp.arange(s), MLA_ROPE)
    ones = jnp.ones((s, MLA_NOPE), F32)
    zpad = jnp.zeros((s, MLA_SLOT - MLA_NOPE - MLA_ROPE), F32)
    cos = jnp.concatenate([ones, jnp.cos(ang), zpad], axis=-1)
    sin = jnp.concatenate([jnp.zeros_like(ones), _sin_signed(ang), zpad], axis=-1)
    return cos, sin


def _gqa_q_order():
    group = GQA_Q_HEADS // GQA_KV_HEADS
    order = []
    for p in range(GQA_KV_HEADS // 2):
        for g in range(group):
            order += [group * (2 * p) + g, group * (2 * p + 1) + g]
    return np.asarray(order)


def _head_cols(order, width):
    return (np.asarray(order)[:, None] * width + np.arange(width)[None, :]).reshape(-1)


def _mla_weights(w_in, w_q_up, w_kv_up):
    d = w_in.shape[0]
    win = jnp.concatenate([w_in, jnp.zeros((d, MLA_LAT_PAD - w_in.shape[1]), w_in.dtype)], axis=1)
    qd = MLA_NOPE + MLA_ROPE
    wq = w_q_up.reshape(MLA_Q_LORA, MLA_HEADS, qd)
    wq = jnp.concatenate([wq, jnp.zeros((MLA_Q_LORA, MLA_HEADS, MLA_SLOT - qd), wq.dtype)], axis=-1)
    wq = wq.reshape(MLA_Q_LORA, MLA_HEADS * MLA_SLOT)
    wkv = w_kv_up.reshape(MLA_KV_LORA, MLA_HEADS, MLA_NOPE + MLA_V)
    k_nope = jnp.concatenate(
        [wkv[..., :MLA_NOPE], jnp.zeros((MLA_KV_LORA, MLA_HEADS, MLA_SLOT - MLA_NOPE), wkv.dtype)], axis=-1)
    place = np.zeros((LANES, MLA_HEADS, MLA_SLOT), np.float32)
    for i in range(MLA_ROPE):
        place[i, :, MLA_NOPE + i] = 1.0
    wk = jnp.concatenate([k_nope.reshape(MLA_KV_LORA, -1),
                          jnp.asarray(place.reshape(LANES, -1), wkv.dtype)], axis=0)
    wv = wkv[..., MLA_NOPE:].reshape(MLA_KV_LORA, MLA_HEADS * MLA_V)
    return win.astype(BF16), wq.astype(BF16), wk.astype(BF16), wv.astype(BF16)


def _slot_gain(g):
    g = jnp.concatenate([g, jnp.zeros((MLA_SLOT - g.shape[0],), g.dtype)])
    return jnp.tile(g, MXU_N // MLA_SLOT)[None, :]


def _trunk(x, mods, p):
    s = x.shape[1]
    for l in range(DEPTH):
        mod = mods[l]
        ng1 = p['norm_g'][l, 0][None, :]
        ng2 = p['norm_g'][l, 1][None, :]
        kind, j = l % 4, l // 4
        if kind == 0:
            n = NA_HEADS * NA_HEAD_DIM
            q, k, v = _qkv_proj(x, mod, ng1, p['na_w_in'][j],
                                jnp.tile(p['na_qk_g'][j, 0], NA_HEADS)[None, :],
                                jnp.tile(p['na_qk_g'][j, 1], NA_HEADS)[None, :],
                                p['gm64'], n, n, n, NA_HEAD_DIM ** -0.5)
            a = _na_attention(q, k, v, p['na_bias'][j])
            wo = p['na_w_out'][j]
        elif kind == 1:
            win, wq, wk, wv = p['mla_w'][j]
            cos, sin = p['mla_tabs'][s]
            q, k, v = _mla_proj(x, mod, ng1, win, p['mla_q_lat_g'][j][None, :],
                                p['mla_kv_lat_g'][j][None, :], wq, wk, wv,
                                _slot_gain(p['mla_qk_g'][j, 0]), _slot_gain(p['mla_qk_g'][j, 1]),
                                p['gm128'], cos, sin, (MLA_NOPE + MLA_ROPE) ** -0.5 * LOG2E)
            a = _mla_flash(q, k, v)
            wo = p['mla_w_out'][j]
        elif kind == 2:
            n = DIFF_HEADS * 2 * DIFF_HEAD_DIM
            lambda_init = 0.8 - 0.6 * math.exp(-0.3 * l)
            q, k, v = _qkv_proj(x, mod, ng1, p['diff_w_in'][j],
                                jnp.tile(p['diff_qk_g'][j, 0], 2 * DIFF_HEADS)[None, :],
                                jnp.tile(p['diff_qk_g'][j, 1], 2 * DIFF_HEADS)[None, :],
                                p['gm64'], n, n, n, DIFF_HEAD_DIM ** -0.5 * LOG2E)
            a = _diff_flash(q, k, v, p['diff_lambda'][j], p['diff_sub_g'][j][None, :], lambda_init)
            wo = p['diff_w_out'][j]
        else:
            nq = GQA_Q_HEADS * GQA_HEAD_DIM
            nkv = GQA_KV_HEADS * GQA_HEAD_DIM
            q, k, v = _qkv_proj(x, mod, ng1, p['gqa_w_in'][j],
                                jnp.tile(p['gqa_qk_g'][j, 0], GQA_Q_HEADS)[None, :],
                                jnp.tile(p['gqa_qk_g'][j, 1], GQA_KV_HEADS)[None, :],
                                p['gm64'], nq, nkv, nkv, GQA_HEAD_DIM ** -0.5 * LOG2E,
                                rope_tabs=p['gqa_tabs'][s])
            a = _gqa_flash(q, k, v)
            wo = p['gqa_w_out'][j]
        x = _post(x, a, mod, wo, ng2, p['ffn_w_in'][l], p['ffn_w_out'][l])
    return x


def kernel(x_prompt, x_sample, c_prompt, c_sample, norm_g, ada_w, ada_b, na_w_in, na_qk_g, na_rpb, na_w_out, mla_w_in, mla_q_lat_g, mla_kv_lat_g, mla_w_q_up, mla_w_kv_up, mla_qk_g, mla_w_out, diff_w_in, diff_qk_g, diff_lambda, diff_sub_g, diff_w_out, gqa_w_in, gqa_qk_g, gqa_w_out, ffn_w_in, ffn_w_out):
    bp, bs = x_prompt.shape[0], x_sample.shape[0]
    d = x_prompt.shape[2]
    rows = -(-(bp + bs) // 8) * 8
    c_all = jnp.concatenate([c_prompt, c_sample, jnp.zeros((rows - bp - bs, d), F32)], axis=0)
    mods = _ada_mod(c_all, ada_w, ada_b)
    mods_p = mods[:, :bp].reshape(DEPTH, bp, 6, d)
    mods_s = mods[:, bp:bp + bs].reshape(DEPTH, bs, 6, d)

    q_cols = _head_cols(_gqa_q_order(), GQA_HEAD_DIM)
    nq = GQA_Q_HEADS * GQA_HEAD_DIM
    gqa_in = jnp.concatenate([gqa_w_in[:, :, :nq][:, :, q_cols], gqa_w_in[:, :, nq:]], axis=2)
    seqs = sorted({x_prompt.shape[1], x_sample.shape[1]})
    p = {
        'norm_g': norm_g,
        'gm64': _block_diag_ones(64), 'gm128': _block_diag_ones(128),
        'na_w_in': na_w_in.astype(BF16), 'na_qk_g': na_qk_g, 'na_w_out': na_w_out.astype(BF16),
        'na_bias': [_na_bias_table(na_rpb[j]) for j in range(na_rpb.shape[0])],
        'mla_w': [_mla_weights(mla_w_in[j], mla_w_q_up[j], mla_w_kv_up[j])
                  for j in range(mla_w_in.shape[0])],
        'mla_tabs': {s: _mla_tables(s) for s in seqs},
        'mla_q_lat_g': mla_q_lat_g, 'mla_kv_lat_g': mla_kv_lat_g, 'mla_qk_g': mla_qk_g,
        'mla_w_out': mla_w_out.astype(BF16),
        'diff_w_in': diff_w_in.astype(BF16), 'diff_qk_g': diff_qk_g, 'diff_lambda': diff_lambda,
        'diff_sub_g': diff_sub_g, 'diff_w_out': diff_w_out.astype(BF16),
        'gqa_w_in': gqa_in.astype(BF16), 'gqa_qk_g': gqa_qk_g,
        'gqa_w_out': gqa_w_out[:, q_cols, :].astype(BF16),
        'gqa_tabs': {s: _axial_tables(s) for s in seqs},
        'ffn_w_in': ffn_w_in.astype(BF16), 'ffn_w_out': ffn_w_out.astype(BF16),
    }
    return (_trunk(x_prompt, mods_p, p), _trunk(x_sample, mods_s, p))
```

```python
import functools
import math

import numpy as np
import jax
import jax.numpy as jnp
from jax import lax
from jax.experimental import pallas as pl
from jax.experimental.pallas import tpu as pltpu

D_MODEL = 1024
DEPTH = 4
GRID_W = 64
EPS = 1e-6
NEG_INF = -1e30
ROPE_THETA = 10000.0
NA_HEADS, NA_HEAD_DIM, NA_KH, NA_KW = 16, 64, 8, 16
MLA_HEADS, MLA_Q_LORA, MLA_KV_LORA, MLA_NOPE, MLA_ROPE, MLA_V = 16, 384, 256, 64, 32, 64
DIFF_HEADS, DIFF_HEAD_DIM = 8, 64
GQA_Q_HEADS, GQA_KV_HEADS, GQA_HEAD_DIM = 16, 4, 64
D_FF = -(-8 * D_MODEL // (3 * 256)) * 256

LANES = 128
MXU_N = 256
ROT_HALF = 16
assert MLA_ROPE == 2 * ROT_HALF and GQA_HEAD_DIM // 2 == 2 * ROT_HALF
LOG2E = math.log2(math.e)
VMEM_LIMIT = 56 * 1024 * 1024

F32 = jnp.float32
BF16 = jnp.bfloat16


def _cparams(sem):
    return pltpu.CompilerParams(dimension_semantics=sem, vmem_limit_bytes=VMEM_LIMIT)


def _const_spec(shape):
    nd = len(shape)
    return pl.BlockSpec(shape, lambda *_: (0,) * nd)


def _dot(a, b):
    return jnp.dot(a, b, preferred_element_type=F32)


def _dot_nt(a, b):
    return lax.dot_general(a, b, (((1,), (1,)), ((), ())), preferred_element_type=F32)


def _ada_kernel(c_ref, w_ref, b_ref, o_ref):
    c = c_ref[...]
    ca = c * (1.0 / (1.0 + jnp.exp(-c)))
    o_ref[0] = jnp.dot(ca, w_ref[0], preferred_element_type=F32,
                       precision=lax.Precision.HIGHEST) + b_ref[0]


def _ada_mod(c_all, ada_w, ada_b):
    rows = c_all.shape[0]
    depth, d, n = ada_w.shape
    tn = 1536
    return pl.pallas_call(
        _ada_kernel,
        grid=(depth, n // tn),
        in_specs=[pl.BlockSpec((rows, d), lambda l, j: (0, 0)),
                  pl.BlockSpec((1, d, tn), lambda l, j: (l, 0, j)),
                  pl.BlockSpec((1, 1, tn), lambda l, j: (l, 0, j))],
        out_specs=pl.BlockSpec((1, rows, tn), lambda l, j: (l, 0, j)),
        out_shape=jax.ShapeDtypeStruct((depth, rows, n), F32),
        compiler_params=_cparams(("arbitrary", "arbitrary")),
        name="ada_mod",
    )(c_all, ada_w, ada_b.reshape(depth, 1, n))


def _norm_mod(x, g, shift, scale):
    ms = jnp.mean(x * x, axis=-1, keepdims=True)
    y = x * lax.rsqrt(ms + EPS)
    return (y * g) * (1.0 + scale) + shift


def _group_rms(a, gmat, inv_n):
    ssq = _dot((a * a).astype(BF16), gmat)
    return a * lax.rsqrt(ssq * inv_n + EPS)


def _rope_chunk(a, cos, sin_signed, first_half):
    fwd = pltpu.roll(a, ROT_HALF, 1)
    bwd = pltpu.roll(a, LANES - ROT_HALF, 1)
    return a * cos + jnp.where(first_half, bwd, fwd) * sin_signed


def _first_half_mask():
    lane = lax.broadcasted_iota(jnp.int32, (1, LANES), 1)
    return (lane & ROT_HALF) == 0


def _qkv_kernel(*refs, n_q, n_k, n_v, q_const, rope):
    if rope:
        (x_ref, mod_ref, ng_ref, w_ref, gq_ref, gk_ref, gm_ref, cos_ref, sin_ref,
         q_ref, k_ref, v_ref) = refs
    else:
        (x_ref, mod_ref, ng_ref, w_ref, gq_ref, gk_ref, gm_ref,
         q_ref, k_ref, v_ref) = refs
    h = _norm_mod(x_ref[0], ng_ref[...], mod_ref[0, 0:1, :], mod_ref[0, 1:2, :])
    hb = h.astype(BF16)
    gmat = gm_ref[...]
    if rope:
        cos = cos_ref[...]
        sin = sin_ref[...]
        fh = _first_half_mask()

    def normed(col0, gain, const):
        a = _dot(hb, w_ref[:, col0:col0 + MXU_N])
        a = _group_rms(a, gmat, 1.0 / 64.0) * (gain * const)
        if rope:
            a = jnp.concatenate(
                [_rope_chunk(a[:, j * LANES:(j + 1) * LANES], cos, sin, fh)
                 for j in range(MXU_N // LANES)], axis=1)
        return a.astype(BF16)

    for c in range(n_q // MXU_N):
        sl = slice(c * MXU_N, (c + 1) * MXU_N)
        q_ref[0, :, sl] = normed(c * MXU_N, gq_ref[:, sl], q_const)
    for c in range(n_k // MXU_N):
        sl = slice(c * MXU_N, (c + 1) * MXU_N)
        k_ref[0, :, sl] = normed(n_q + c * MXU_N, gk_ref[:, sl], 1.0)
    for c in range(n_v // MXU_N):
        sl = slice(c * MXU_N, (c + 1) * MXU_N)
        v_ref[0, :, sl] = _dot(hb, w_ref[:, n_q + n_k + c * MXU_N:
                                          n_q + n_k + (c + 1) * MXU_N]).astype(BF16)


def _qkv_proj(x, mod, ng, w, gq, gk, gmat, n_q, n_k, n_v, q_const, rope_tabs=None, tm=512):
    b, s, d = x.shape
    n = n_q + n_k + n_v
    rope = rope_tabs is not None
    in_specs = [pl.BlockSpec((1, tm, d), lambda i, j: (i, j, 0)),
                pl.BlockSpec((1, 6, d), lambda i, j: (i, 0, 0)),
                _const_spec((1, d)),
                _const_spec((d, n)),
                _const_spec((1, n_q)),
                _const_spec((1, n_k)),
                _const_spec((MXU_N, MXU_N))]
    args = [x, mod, ng, w, gq, gk, gmat]
    if rope:
        in_specs += [pl.BlockSpec((tm, LANES), lambda i, j: (j, 0))] * 2
        args += list(rope_tabs)
    outs = [jax.ShapeDtypeStruct((b, s, m), BF16) for m in (n_q, n_k, n_v)]
    out_specs = [pl.BlockSpec((1, tm, m), lambda i, j: (i, j, 0)) for m in (n_q, n_k, n_v)]
    return pl.pallas_call(
        functools.partial(_qkv_kernel, n_q=n_q, n_k=n_k, n_v=n_v, q_const=q_const, rope=rope),
        grid=(b, s // tm),
        in_specs=in_specs,
        out_specs=out_specs,
        out_shape=outs,
        compiler_params=_cparams(("parallel", "parallel")),
        name="qkv_proj",
    )(*args)


MLA_LAT_PAD = 768
MLA_SLOT = 128


def _mla_proj_kernel(x_ref, mod_ref, ng_ref, win_ref, gql_ref, gkvl_ref, wq_ref, wk_ref, wv_ref,
                     gq_ref, gk_ref, gm_ref, cos_ref, sin_ref, q_ref, k_ref, v_ref, *, q_const):
    h = _norm_mod(x_ref[0], ng_ref[...], mod_ref[0, 0:1, :], mod_ref[0, 1:2, :])
    lat = _dot(h.astype(BF16), win_ref[...])
    q_lat = lat[:, :MLA_Q_LORA]
    kv_lat = lat[:, MLA_Q_LORA:MLA_Q_LORA + MLA_KV_LORA]
    k_rope = lat[:, MLA_Q_LORA + MLA_KV_LORA:]
    qn = q_lat * lax.rsqrt(jnp.mean(q_lat * q_lat, axis=-1, keepdims=True) + EPS) * gql_ref[...]
    kvn = kv_lat * lax.rsqrt(jnp.mean(kv_lat * kv_lat, axis=-1, keepdims=True) + EPS) * gkvl_ref[...]
    qnb = qn.astype(BF16)
    kvnb = kvn.astype(BF16)
    kvr = jnp.concatenate([kvnb, k_rope.astype(BF16)], axis=1)
    gmat = gm_ref[...]
    cos = cos_ref[...]
    sin = sin_ref[...]
    fh = _first_half_mask()
    inv_n = 1.0 / (MLA_NOPE + MLA_ROPE)
    n_slots = MLA_HEADS * MLA_SLOT

    def finish(a, gain, const):
        a = _group_rms(a, gmat, inv_n) * (gain * const)
        a = jnp.concatenate(
            [_rope_chunk(a[:, j * LANES:(j + 1) * LANES], cos, sin, fh)
             for j in range(MXU_N // LANES)], axis=1)
        return a.astype(BF16)

    for c in range(n_slots // MXU_N):
        sl = slice(c * MXU_N, (c + 1) * MXU_N)
        q_ref[0, :, sl] = finish(_dot(qnb, wq_ref[:, sl]), gq_ref[...], q_const)
        k_ref[0, :, sl] = finish(_dot(kvr, wk_ref[:, sl]), gk_ref[...], 1.0)
    for c in range(MLA_HEADS * MLA_V // MXU_N):
        sl = slice(c * MXU_N, (c + 1) * MXU_N)
        v_ref[0, :, sl] = _dot(kvnb, wv_ref[:, sl]).astype(BF16)


def _mla_proj(x, mod, ng, win, gql, gkvl, wq, wk, wv, gq, gk, gmat, cos, sin, q_const, tm=512):
    b, s, d = x.shape
    n_slots = MLA_HEADS * MLA_SLOT
    n_v = MLA_HEADS * MLA_V
    in_specs = [pl.BlockSpec((1, tm, d), lambda i, j: (i, j, 0)),
                pl.BlockSpec((1, 6, d), lambda i, j: (i, 0, 0)),
                _const_spec((1, d)),
                _const_spec(win.shape),
                _const_spec(gql.shape),
                _const_spec(gkvl.shape),
                _const_spec(wq.shape),
                _const_spec(wk.shape),
                _const_spec(wv.shape),
                _const_spec(gq.shape),
                _const_spec(gk.shape),
                _const_spec((MXU_N, MXU_N)),
                pl.BlockSpec((tm, LANES), lambda i, j: (j, 0)),
                pl.BlockSpec((tm, LANES), lambda i, j: (j, 0))]
    outs = [jax.ShapeDtypeStruct((b, s, m), BF16) for m in (n_slots, n_slots, n_v)]
    out_specs = [pl.BlockSpec((1, tm, m), lambda i, j: (i, j, 0)) for m in (n_slots, n_slots, n_v)]
    return pl.pallas_call(
        functools.partial(_mla_proj_kernel, q_const=q_const),
        grid=(b, s // tm),
        in_specs=in_specs,
        out_specs=out_specs,
        out_shape=outs,
        compiler_params=_cparams(("parallel", "parallel")),
        name="mla_proj",
    )(x, mod, ng, win, gql, gkvl, wq, wk, wv, gq, gk, gmat, cos, sin)


HEAD_LANES = 64


def _lo_mask():
    return lax.broadcasted_iota(jnp.int32, (1, LANES), 1) < HEAD_LANES


def _chunk_start(c, tkc):
    return c * tkc if isinstance(c, int) else pl.multiple_of(c * tkc, tkc)


def _flash_loop(*, n_chunks, n_streams, scores, pv_rhs, m_ref, acc_ref, p_refs, a_refs, fresh=None):
    lo = _lo_mask()

    def init():
        m_ref[...] = jnp.full(m_ref.shape, NEG_INF, F32)
        acc_ref[...] = jnp.zeros(acc_ref.shape, F32)

    _run_when(fresh, init)

    def softmax_stage(i, slot):
        for h in range(n_streams):
            s, shift = scores(i, h)
            m_prev = m_ref[h]
            row_max = jnp.max(s, axis=1, keepdims=True)
            if shift is not None:
                row_max = row_max + shift
            m_next = jnp.maximum(m_prev, row_max)
            a_refs[slot][h] = jnp.exp2(m_prev - m_next)
            m_ref[h] = m_next
            m_eff = m_next if shift is None else m_next - shift
            m_rep = jnp.concatenate([m_eff] * (s.shape[1] // LANES), axis=1)
            p_refs[slot][h] = jnp.exp2((s - m_rep).astype(BF16))

    def pv_stage(i, slot):
        pv = None
        for h in range(n_streams):
            t = _dot(p_refs[slot][h], pv_rhs(i, h))
            pv = t if pv is None else pv + t
        a = a_refs[slot][0]
        if n_streams == 2:
            a = jnp.where(lo, a, a_refs[slot][1])
        acc_ref[...] = acc_ref[...] * jnp.concatenate([a, a], axis=1) + pv

    def body(j, carry):
        softmax_stage(2 * j, 0)
        pv_stage(2 * j - 1, 1)
        softmax_stage(2 * j + 1, 1)
        pv_stage(2 * j, 0)
        return carry

    assert n_chunks % 2 == 0
    softmax_stage(0, 0)
    softmax_stage(1, 1)
    pv_stage(0, 0)
    lax.fori_loop(1, n_chunks // 2, body, 0, unroll=4)
    pv_stage(n_chunks - 1, 1)


FLASH_KEYS_PER_STEP = 8192


def _key_spans(s):
    return max(1, s // FLASH_KEYS_PER_STEP)


def _key_span_flags(n_spans):
    if n_spans == 1:
        return None, None
    span = pl.program_id(3)
    return span == 0, span == n_spans - 1


def _run_when(cond, fn):
    if cond is None:
        fn()
    else:
        pl.when(cond)(fn)


def _pair_pv_rhs(v, h):
    msk = _half_mask(v.shape, h)
    rhs = jnp.concatenate([jnp.where(msk, v.astype(F32), 0.0), jnp.where(msk, 1.0, 0.0)], axis=1)
    return rhs.astype(BF16)


def _half_mask(shape, h):
    lane = lax.broadcasted_iota(jnp.int32, shape, len(shape) - 1)
    return (lane < HEAD_LANES) if h == 0 else (lane >= HEAD_LANES)


def _mask_half(x, h):
    return jnp.where(_half_mask(x.shape, h), x.astype(F32), 0.0).astype(BF16)


def _flash_scratch(n_streams, m, tkc):
    return [pltpu.VMEM((n_streams, m, LANES), F32),
            pltpu.VMEM((m, 2 * LANES), F32),
            pltpu.VMEM((n_streams, m, tkc), BF16),
            pltpu.VMEM((n_streams, m, tkc), BF16),
            pltpu.VMEM((n_streams, m, LANES), F32),
            pltpu.VMEM((n_streams, m, LANES), F32)]


def _gqa_flash_kernel(q_ref, k_ref, v_ref, o_ref, m_ref, acc_ref, p0_ref, p1_ref, a0_ref, a1_ref,
                      *, tq, tkc, group, n_spans):
    q = q_ref[0]
    qs = [jnp.concatenate([_mask_half(q[:, g * LANES:(g + 1) * LANES], h)
                           for g in range(group)], axis=0)
          for h in range(2)]

    def scores(c, h):
        return _dot_nt(qs[h], k_ref[0, pl.ds(_chunk_start(c, tkc), tkc), :]), None

    def pv_rhs(c, h):
        return _pair_pv_rhs(v_ref[0, pl.ds(_chunk_start(c, tkc), tkc), :], h)

    first, last = _key_span_flags(n_spans)
    _flash_loop(n_chunks=k_ref.shape[1] // tkc, n_streams=2, scores=scores, pv_rhs=pv_rhs,
                m_ref=m_ref, acc_ref=acc_ref, p_refs=(p0_ref, p1_ref), a_refs=(a0_ref, a1_ref),
                fresh=first)

    def finalize():
        o = acc_ref[:, :LANES] / acc_ref[:, LANES:]
        for g in range(group):
            o_ref[0, :, g * LANES:(g + 1) * LANES] = o[g * tq:(g + 1) * tq].astype(BF16)

    _run_when(last, finalize)


def _gqa_flash(q, k, v, tq=256, tkc=512):
    b, s, nq = q.shape
    pairs = k.shape[2] // LANES
    group = nq // k.shape[2]
    wq = group * LANES
    tkc = min(tkc, s // 2)
    n_spans = _key_spans(s)
    span = s // n_spans
    return pl.pallas_call(
        functools.partial(_gqa_flash_kernel, tq=tq, tkc=tkc, group=group, n_spans=n_spans),
        grid=(b, pairs, s // tq, n_spans),
        in_specs=[pl.BlockSpec((1, tq, wq), lambda i, p, a, c: (i, a, p)),
                  pl.BlockSpec((1, span, LANES), lambda i, p, a, c: (i, c, p)),
                  pl.BlockSpec((1, span, LANES), lambda i, p, a, c: (i, c, p))],
        out_specs=pl.BlockSpec((1, tq, wq), lambda i, p, a, c: (i, a, p)),
        out_shape=jax.ShapeDtypeStruct((b, s, nq), BF16),
        scratch_shapes=_flash_scratch(2, group * tq, tkc),
        compiler_params=_cparams(("parallel", "parallel", "parallel", "arbitrary")),
        name="gqa_flash",
    )(q, k, v)


def _mla_flash_kernel(q_ref, k_ref, v_ref, o_ref, m_ref, acc_ref, p0_ref, p1_ref, a0_ref, a1_ref,
                      *, tkc, n_spans):
    qs = [q_ref[0, :, :LANES], q_ref[0, :, LANES:]]

    def scores(c, h):
        k = k_ref[0, pl.ds(_chunk_start(c, tkc), tkc), h * LANES:(h + 1) * LANES]
        return _dot_nt(qs[h], k), None

    def pv_rhs(c, h):
        return _pair_pv_rhs(v_ref[0, pl.ds(_chunk_start(c, tkc), tkc), :], h)

    first, last = _key_span_flags(n_spans)
    _flash_loop(n_chunks=k_ref.shape[1] // tkc, n_streams=2, scores=scores, pv_rhs=pv_rhs,
                m_ref=m_ref, acc_ref=acc_ref, p_refs=(p0_ref, p1_ref), a_refs=(a0_ref, a1_ref),
                fresh=first)

    def finalize():
        o_ref[0] = (acc_ref[:, :LANES] / acc_ref[:, LANES:]).astype(BF16)

    _run_when(last, finalize)


def _mla_flash(q, k, v, tq=1024, tkc=512):
    b, s, nv = v.shape
    pairs = nv // LANES
    tkc = min(tkc, s // 2)
    n_spans = _key_spans(s)
    span = s // n_spans
    return pl.pallas_call(
        functools.partial(_mla_flash_kernel, tkc=tkc, n_spans=n_spans),
        grid=(b, pairs, s // tq, n_spans),
        in_specs=[pl.BlockSpec((1, tq, 2 * LANES), lambda i, p, a, c: (i, a, p)),
                  pl.BlockSpec((1, span, 2 * LANES), lambda i, p, a, c: (i, c, p)),
                  pl.BlockSpec((1, span, LANES), lambda i, p, a, c: (i, c, p))],
        out_specs=pl.BlockSpec((1, tq, LANES), lambda i, p, a, c: (i, a, p)),
        out_shape=jax.ShapeDtypeStruct((b, s, nv), BF16),
        scratch_shapes=_flash_scratch(2, tq, tkc),
        compiler_params=_cparams(("parallel", "parallel", "parallel", "arbitrary")),
        name="mla_flash",
    )(q, k, v)


ALIBI_PIECES = 3
POS_SPLIT = 32


def _diff_flash_kernel(q_ref, k_ref, v_ref, kpos_ref, lam_ref, subg_ref, o_ref, m_ref, acc_ref,
                       p0_ref, p1_ref, a0_ref, a1_ref, qx_ref, row_ref, *, tq, lambda_init, n_spans):
    h = pl.program_id(1)
    d = pl.program_id(2)
    n = k_ref.shape[1] // tq
    q = q_ref[0]
    qs = jnp.concatenate([_mask_half(q, 0), _mask_half(q, 1)], axis=0)
    slope = jnp.exp2(-(8.0 / DIFF_HEADS) * (h + 1).astype(F32) * jnp.ones((1, 1), F32)) * LOG2E
    pieces = []
    rest = slope
    for _ in range(ALIBI_PIECES):
        piece = rest.astype(BF16).astype(F32)
        pieces.append(piece)
        rest = rest - piece
    lane = lax.broadcasted_iota(jnp.int32, (1, LANES), 1)
    aug = jnp.zeros((1, LANES), F32)
    for j, piece in enumerate(pieces):
        aug = jnp.where((lane == j) | (lane == j + ALIBI_PIECES), piece, aug)
    first, last = _key_span_flags(n_spans)

    def build_query_side():
        for side, sign in enumerate((1.0, -1.0)):
            aug_rows = jnp.broadcast_to(sign * aug, (2 * tq, LANES)).astype(BF16)
            qx_ref[side] = jnp.concatenate([qs, aug_rows], axis=1)
        r = lax.broadcasted_iota(jnp.int32, (2 * tq, LANES), 0) & (tq - 1)
        row_ref[...] = r.astype(F32) * slope

    _run_when(first, build_query_side)

    base = 0 if n_spans == 1 else pl.program_id(3) * n
    start = jnp.where((d >= base) & (d < base + n), d - base, 0)

    def local_chunk(i):
        return start if isinstance(i, int) and i == 0 else jnp.remainder(start + i, n)

    def scores(i, _):
        cl = local_chunk(i)
        c = base + cl
        k = k_ref[0, pl.ds(pl.multiple_of(cl * tq, tq), tq), :]
        if isinstance(i, int) and i == 0:
            rel = (lax.broadcasted_iota(jnp.int32, (tq, tq), 0)
                   - lax.broadcasted_iota(jnp.int32, (tq, tq), 1))
            bias = jnp.abs(rel + (d - c) * tq).astype(F32) * slope
            return _dot_nt(qs, k) - jnp.concatenate([bias, bias], axis=0), None
        before = c < d
        s = _dot_nt(qx_ref[jnp.where(before, 0, 1)], jnp.concatenate([k, kpos_ref[...]], axis=1))
        far = (jnp.abs(d - c) * tq).astype(F32)
        shift = -(slope * far) - jnp.where(before, 1.0, -1.0) * row_ref[...]
        return s, shift

    def pv_rhs(i, _):
        v = v_ref[0, pl.ds(pl.multiple_of(local_chunk(i) * tq, tq), tq), :]
        return jnp.concatenate([v, jnp.ones(v.shape, BF16)], axis=1)

    _flash_loop(n_chunks=n, n_streams=1, scores=scores, pv_rhs=pv_rhs,
                m_ref=m_ref, acc_ref=acc_ref, p_refs=(p0_ref, p1_ref), a_refs=(a0_ref, a1_ref),
                fresh=first)

    def finalize():
        lp = lam_ref[...]
        lam = (jnp.exp(jnp.sum(lp[0:1] * lp[1:2], axis=1, keepdims=True))
               - jnp.exp(jnp.sum(lp[2:3] * lp[3:4], axis=1, keepdims=True)) + lambda_init)
        o = acc_ref[:, :LANES] / acc_ref[:, LANES:]
        o = o[:tq] - lam * o[tq:]
        o = o * lax.rsqrt(jnp.mean(o * o, axis=-1, keepdims=True) + EPS) * subg_ref[...]
        o_ref[0] = (o * (1.0 - lambda_init)).astype(BF16)

    _run_when(last, finalize)


def _diff_kpos_table(tq):
    c = np.arange(tq)
    tab = np.zeros((tq, LANES), np.float32)
    tab[:, :ALIBI_PIECES] = (POS_SPLIT * (c // POS_SPLIT))[:, None]
    tab[:, ALIBI_PIECES:2 * ALIBI_PIECES] = (c % POS_SPLIT)[:, None]
    return jnp.asarray(tab, BF16)


def _diff_flash(q, k, v, lam_p, sub_g, lambda_init, tq=512):
    b, s, n = v.shape
    heads = n // LANES
    tq = min(tq, s // 2)
    assert tq & (tq - 1) == 0 and tq <= POS_SPLIT * 256
    n_spans = _key_spans(s)
    span = s // n_spans
    return pl.pallas_call(
        functools.partial(_diff_flash_kernel, tq=tq, lambda_init=lambda_init, n_spans=n_spans),
        grid=(b, heads, s // tq, n_spans),
        in_specs=[pl.BlockSpec((1, tq, LANES), lambda i, p, a, c: (i, a, p)),
                  pl.BlockSpec((1, span, LANES), lambda i, p, a, c: (i, c, p)),
                  pl.BlockSpec((1, span, LANES), lambda i, p, a, c: (i, c, p)),
                  _const_spec((tq, LANES)),
                  _const_spec(lam_p.shape),
                  _const_spec(sub_g.shape)],
        out_specs=pl.BlockSpec((1, tq, LANES), lambda i, p, a, c: (i, a, p)),
        out_shape=jax.ShapeDtypeStruct((b, s, n), BF16),
        scratch_shapes=_flash_scratch(1, 2 * tq, tq) + [pltpu.VMEM((2, 2 * tq, 2 * LANES), BF16),
                                                        pltpu.VMEM((2 * tq, LANES), F32)],
        compiler_params=_cparams(("parallel", "parallel", "parallel", "arbitrary")),
        name="diff_flash",
    )(q, k, v, _diff_kpos_table(tq), lam_p, sub_g)


NA_ROWS_PER_STEP = 8
NA_BLOCK = NA_ROWS_PER_STEP * GRID_W
NA_WIN = NA_KH * GRID_W


def _na_kernel(q_ref, kp_ref, kc_ref, kn_ref, vp_ref, vc_ref, vn_ref, bias_ref, o_ref,
               kw_ref, vw_ref, *, rows):
    blk = pl.program_id(2)
    kw_ref[0:NA_BLOCK] = kp_ref[0]
    kw_ref[NA_BLOCK:2 * NA_BLOCK] = kc_ref[0]
    kw_ref[2 * NA_BLOCK:] = kn_ref[0]
    vw_ref[0:NA_BLOCK] = vp_ref[0]
    vw_ref[NA_BLOCK:2 * NA_BLOCK] = vc_ref[0]
    vw_ref[2 * NA_BLOCK:] = vn_ref[0]
    lo = _lo_mask()
    for rl in range(NA_ROWS_PER_STEP):
        r = blk * NA_ROWS_PER_STEP + rl
        r0 = jnp.clip(r - NA_KH // 2, 0, rows - NA_KH)
        start = pl.multiple_of((r0 - blk * NA_ROWS_PER_STEP + NA_ROWS_PER_STEP) * GRID_W, GRID_W)
        off = r0 - r + (NA_KH - 1)
        kwin = kw_ref[pl.ds(start, NA_WIN), :]
        vwin = vw_ref[pl.ds(start, NA_WIN), :]
        qr = q_ref[0, rl * GRID_W:(rl + 1) * GRID_W, :]
        qs = jnp.concatenate([_mask_half(qr, 0), _mask_half(qr, 1)], axis=0)
        bias = jnp.concatenate(
            [jnp.concatenate([bias_ref[half, off + 2 * j] for j in range(NA_KH // 2)], axis=1)
             for half in range(2)], axis=0)
        s = _dot_nt(qs, kwin) + bias
        p = jnp.exp(s - jnp.max(s, axis=1, keepdims=True))
        o = _dot(p.astype(BF16), vwin) / jnp.sum(p, axis=1, keepdims=True)
        o_ref[0, rl * GRID_W:(rl + 1) * GRID_W, :] = jnp.where(lo, o[:GRID_W], o[GRID_W:]).astype(BF16)


def _na_attention(q, k, v, bias_tab):
    b, s, n = q.shape
    pairs = n // LANES
    rows = s // GRID_W
    nblk = s // NA_BLOCK
    blk_spec = lambda f: pl.BlockSpec((1, NA_BLOCK, LANES), f)
    prev = lambda i, p, a: (i, jnp.maximum(a - 1, 0), p)
    cur = lambda i, p, a: (i, a, p)
    nxt = lambda i, p, a: (i, jnp.minimum(a + 1, nblk - 1), p)
    return pl.pallas_call(
        functools.partial(_na_kernel, rows=rows),
        grid=(b, pairs, nblk),
        in_specs=[blk_spec(cur), blk_spec(prev), blk_spec(cur), blk_spec(nxt),
                  blk_spec(prev), blk_spec(cur), blk_spec(nxt),
                  pl.BlockSpec((2,) + bias_tab.shape[1:], lambda i, p, a: (p, 0, 0, 0))],
        out_specs=blk_spec(cur),
        out_shape=jax.ShapeDtypeStruct((b, s, n), BF16),
        scratch_shapes=[pltpu.VMEM((3 * NA_BLOCK, LANES), BF16),
                        pltpu.VMEM((3 * NA_BLOCK, LANES), BF16)],
        compiler_params=_cparams(("parallel", "parallel", "parallel")),
        name="na_attention",
    )(q, k, k, k, v, v, v, bias_tab)


def _na_bias_table(rpb):
    col = np.arange(GRID_W)
    col_start = np.clip(col - NA_KW // 2, 0, GRID_W - NA_KW)
    valid = (col[None, :] >= col_start[:, None]) & (col[None, :] < col_start[:, None] + NA_KW)
    dc = np.clip(col[None, :] - col[:, None] + NA_KW - 1, 0, 2 * NA_KW - 2)
    t = rpb.astype(F32)[:, :, dc]
    t = jnp.where(jnp.asarray(valid)[None, None], t, NEG_INF)
    return jnp.concatenate([t[:, :-1], t[:, 1:]], axis=-1)


FF_CHUNK = 256


def _post_kernel(x_ref, a_ref, mod_ref, wo_ref, ng_ref, w1_ref, w2_ref, o_ref):
    x = x_ref[0]
    x = x + mod_ref[0, 2:3, :] * _dot(a_ref[0], wo_ref[...])
    h = _norm_mod(x, ng_ref[...], mod_ref[0, 3:4, :], mod_ref[0, 4:5, :]).astype(BF16)
    y = None
    for c in range(D_FF // FF_CHUNK):
        g = _dot(h, w1_ref[:, c * FF_CHUNK:(c + 1) * FF_CHUNK])
        u = _dot(h, w1_ref[:, D_FF + c * FF_CHUNK:D_FF + (c + 1) * FF_CHUNK])
        act = (g * (1.0 / (1.0 + jnp.exp(-g))) * u).astype(BF16)
        part = _dot(act, w2_ref[c * FF_CHUNK:(c + 1) * FF_CHUNK, :])
        y = part if y is None else y + part
    o_ref[0] = x + mod_ref[0, 5:6, :] * y


def _post(x, attn, mod, wo, ng, w1, w2, tm=512):
    b, s, d = x.shape
    single = pl.Buffered(1)
    return pl.pallas_call(
        _post_kernel,
        grid=(b, s // tm),
        in_specs=[pl.BlockSpec((1, tm, d), lambda i, j: (i, j, 0)),
                  pl.BlockSpec((1, tm, attn.shape[2]), lambda i, j: (i, j, 0)),
                  pl.BlockSpec((1, 6, d), lambda i, j: (i, 0, 0)),
                  pl.BlockSpec(wo.shape, lambda i, j: (0, 0), pipeline_mode=single),
                  _const_spec((1, d)),
                  pl.BlockSpec(w1.shape, lambda i, j: (0, 0), pipeline_mode=single),
                  pl.BlockSpec(w2.shape, lambda i, j: (0, 0), pipeline_mode=single)],
        out_specs=pl.BlockSpec((1, tm, d), lambda i, j: (i, j, 0)),
        out_shape=jax.ShapeDtypeStruct((b, s, d), F32),
        compiler_params=_cparams(("parallel", "parallel")),
        name="post_ffn",
    )(x, attn, mod, wo, ng, w1, w2)


def _block_diag_ones(block):
    idx = np.arange(MXU_N) // block
    return jnp.asarray(idx[:, None] == idx[None, :], dtype=BF16)


def _rope_angles(pos, dim):
    inv = ROPE_THETA ** (-jnp.arange(0, dim, 2, dtype=F32) / dim)
    ang = pos.astype(F32)[:, None] * inv[None, :]
    return jnp.concatenate([ang, ang], axis=-1)


def _sin_signed(ang):
    half = ang.shape[-1] // 2
    sign = jnp.concatenate([-jnp.ones((half,), F32), jnp.ones((half,), F32)])
    return jnp.sin(ang) * sign


def _axial_tables(s):
    t = jnp.arange(s)
    half = GQA_HEAD_DIM // 2
    ar, ac = _rope_angles(t // GRID_W, half), _rope_angles(t % GRID_W, half)
    cos = jnp.concatenate([jnp.cos(ar), jnp.cos(ac)], axis=-1)
    sin = jnp.concatenate([_sin_signed(ar), _sin_signed(ac)], axis=-1)
    return jnp.tile(cos, (1, 2)), jnp.tile(sin, (1, 2))


def _mla_tables(s):
    ang = _rope_angles(jnp.arange(s), MLA_ROPE)
    ones = jnp.ones((s, MLA_NOPE), F32)
    zpad = jnp.zeros((s, MLA_SLOT - MLA_NOPE - MLA_ROPE), F32)
    cos = jnp.concatenate([ones, jnp.cos(ang), zpad], axis=-1)
    sin = jnp.concatenate([jnp.zeros_like(ones), _sin_signed(ang), zpad], axis=-1)
    return cos, sin


def _gqa_q_order():
    group = GQA_Q_HEADS // GQA_KV_HEADS
    order = []
    for p in range(GQA_KV_HEADS // 2):
        for g in range(group):
            order += [group * (2 * p) + g, group * (2 * p + 1) + g]
    return np.asarray(order)


def _head_cols(order, width):
    return (np.asarray(order)[:, None] * width + np.arange(width)[None, :]).reshape(-1)


def _mla_weights(w_in, w_q_up, w_kv_up):
    d = w_in.shape[0]
    win = jnp.concatenate([w_in, jnp.zeros((d, MLA_LAT_PAD - w_in.shape[1]), w_in.dtype)], axis=1)
    qd = MLA_NOPE + MLA_ROPE
    wq = w_q_up.reshape(MLA_Q_LORA, MLA_HEADS, qd)
    wq = jnp.concatenate([wq, jnp.zeros((MLA_Q_LORA, MLA_HEADS, MLA_SLOT - qd), wq.dtype)], axis=-1)
    wq = wq.reshape(MLA_Q_LORA, MLA_HEADS * MLA_SLOT)
    wkv = w_kv_up.reshape(MLA_KV_LORA, MLA_HEADS, MLA_NOPE + MLA_V)
    k_nope = jnp.concatenate(
        [wkv[..., :MLA_NOPE], jnp.zeros((MLA_KV_LORA, MLA_HEADS, MLA_SLOT - MLA_NOPE), wkv.dtype)], axis=-1)
    place = np.zeros((LANES, MLA_HEADS, MLA_SLOT), np.float32)
    for i in range(MLA_ROPE):
        place[i, :, MLA_NOPE + i] = 1.0
    wk = jnp.concatenate([k_nope.reshape(MLA_KV_LORA, -1),
                          jnp.asarray(place.reshape(LANES, -1), wkv.dtype)], axis=0)
    wv = wkv[..., MLA_NOPE:].reshape(MLA_KV_LORA, MLA_HEADS * MLA_V)
    return win.astype(BF16), wq.astype(BF16), wk.astype(BF16), wv.astype(BF16)


def _slot_gain(g):
    g = jnp.concatenate([g, jnp.zeros((MLA_SLOT - g.shape[0],), g.dtype)])
    return jnp.tile(g, MXU_N // MLA_SLOT)[None, :]


def _trunk(x, mods, p):
    s = x.shape[1]
    for l in range(DEPTH):
        mod = mods[l]
        ng1 = p['norm_g'][l, 0][None, :]
        ng2 = p['norm_g'][l, 1][None, :]
        kind, j = l % 4, l // 4
        if kind == 0:
            n = NA_HEADS * NA_HEAD_DIM
            q, k, v = _qkv_proj(x, mod, ng1, p['na_w_in'][j],
                                jnp.tile(p['na_qk_g'][j, 0], NA_HEADS)[None, :],
                                jnp.tile(p['na_qk_g'][j, 1], NA_HEADS)[None, :],
                                p['gm64'], n, n, n, NA_HEAD_DIM ** -0.5)
            a = _na_attention(q, k, v, p['na_bias'][j])
            wo = p['na_w_out'][j]
        elif kind == 1:
            win, wq, wk, wv = p['mla_w'][j]
            cos, sin = p['mla_tabs'][s]
            q, k, v = _mla_proj(x, mod, ng1, win, p['mla_q_lat_g'][j][None, :],
                                p['mla_kv_lat_g'][j][None, :], wq, wk, wv,
                                _slot_gain(p['mla_qk_g'][j, 0]), _slot_gain(p['mla_qk_g'][j, 1]),
                                p['gm128'], cos, sin, (MLA_NOPE + MLA_ROPE) ** -0.5 * LOG2E)
            a = _mla_flash(q, k, v)
            wo = p['mla_w_out'][j]
        elif kind == 2:
            n = DIFF_HEADS * 2 * DIFF_HEAD_DIM
            lambda_init = 0.8 - 0.6 * math.exp(-0.3 * l)
            q, k, v = _qkv_proj(x, mod, ng1, p['diff_w_in'][j],
                                jnp.tile(p['diff_qk_g'][j, 0], 2 * DIFF_HEADS)[None, :],
                                jnp.tile(p['diff_qk_g'][j, 1], 2 * DIFF_HEADS)[None, :],
                                p['gm64'], n, n, n, DIFF_HEAD_DIM ** -0.5 * LOG2E)
            a = _diff_flash(q, k, v, p['diff_lambda'][j], p['diff_sub_g'][j][None, :], lambda_init)
            wo = p['diff_w_out'][j]
        else:
            nq = GQA_Q_HEADS * GQA_HEAD_DIM
            nkv = GQA_KV_HEADS * GQA_HEAD_DIM
            q, k, v = _qkv_proj(x, mod, ng1, p['gqa_w_in'][j],
                                jnp.tile(p['gqa_qk_g'][j, 0], GQA_Q_HEADS)[None, :],
                                jnp.tile(p['gqa_qk_g'][j, 1], GQA_KV_HEADS)[None, :],
                                p['gm64'], nq, nkv, nkv, GQA_HEAD_DIM ** -0.5 * LOG2E,
                                rope_tabs=p['gqa_tabs'][s])
            a = _gqa_flash(q, k, v)
            wo = p['gqa_w_out'][j]
        x = _post(x, a, mod, wo, ng2, p['ffn_w_in'][l], p['ffn_w_out'][l])
    return x


def kernel(x_prompt, x_sample, c_prompt, c_sample, norm_g, ada_w, ada_b, na_w_in, na_qk_g, na_rpb, na_w_out, mla_w_in, mla_q_lat_g, mla_kv_lat_g, mla_w_q_up, mla_w_kv_up, mla_qk_g, mla_w_out, diff_w_in, diff_qk_g, diff_lambda, diff_sub_g, diff_w_out, gqa_w_in, gqa_qk_g, gqa_w_out, ffn_w_in, ffn_w_out):
    bp, bs = x_prompt.shape[0], x_sample.shape[0]
    d = x_prompt.shape[2]
    rows = -(-(bp + bs) // 8) * 8
    c_all = jnp.concatenate([c_prompt, c_sample, jnp.zeros((rows - bp - bs, d), F32)], axis=0)
    mods = _ada_mod(c_all, ada_w, ada_b)
    mods_p = mods[:, :bp].reshape(DEPTH, bp, 6, d)
    mods_s = mods[:, bp:bp + bs].reshape(DEPTH, bs, 6, d)

    q_cols = _head_cols(_gqa_q_order(), GQA_HEAD_DIM)
    nq = GQA_Q_HEADS * GQA_HEAD_DIM
    gqa_in = jnp.concatenate([gqa_w_in[:, :, :nq][:, :, q_cols], gqa_w_in[:, :, nq:]], axis=2)
    seqs = sorted({x_prompt.shape[1], x_sample.shape[1]})
    p = {
        'norm_g': norm_g,
        'gm64': _block_diag_ones(64), 'gm128': _block_diag_ones(128),
        'na_w_in': na_w_in.astype(BF16), 'na_qk_g': na_qk_g, 'na_w_out': na_w_out.astype(BF16),
        'na_bias': [_na_bias_table(na_rpb[j]) for j in range(na_rpb.shape[0])],
        'mla_w': [_mla_weights(mla_w_in[j], mla_w_q_up[j], mla_w_kv_up[j])
                  for j in range(mla_w_in.shape[0])],
        'mla_tabs': {s: _mla_tables(s) for s in seqs},
        'mla_q_lat_g': mla_q_lat_g, 'mla_kv_lat_g': mla_kv_lat_g, 'mla_qk_g': mla_qk_g,
        'mla_w_out': mla_w_out.astype(BF16),
        'diff_w_in': diff_w_in.astype(BF16), 'diff_qk_g': diff_qk_g, 'diff_lambda': diff_lambda,
        'diff_sub_g': diff_sub_g, 'diff_w_out': diff_w_out.astype(BF16),
        'gqa_w_in': gqa_in.astype(BF16), 'gqa_qk_g': gqa_qk_g,
        'gqa_w_out': gqa_w_out[:, q_cols, :].astype(BF16),
        'gqa_tabs': {s: _axial_tables(s) for s in seqs},
        'ffn_w_in': ffn_w_in.astype(BF16), 'ffn_w_out': ffn_w_out.astype(BF16),
    }
    return (_trunk(x_prompt, mods_p, p), _trunk(x_sample, mods_s, p))
```

```python
import functools
import math

import numpy as np
import jax
import jax.numpy as jnp
from jax import lax
from jax.experimental import pallas as pl
from jax.experimental.pallas import tpu as pltpu

D_MODEL = 1024
DEPTH = 4
GRID_W = 64
EPS = 1e-6
NEG_INF = -1e30
ROPE_THETA = 10000.0
NA_HEADS, NA_HEAD_DIM, NA_KH, NA_KW = 16, 64, 8, 16
MLA_HEADS, MLA_Q_LORA, MLA_KV_LORA, MLA_NOPE, MLA_ROPE, MLA_V = 16, 384, 256, 64, 32, 64
DIFF_HEADS, DIFF_HEAD_DIM = 8, 64
GQA_Q_HEADS, GQA_KV_HEADS, GQA_HEAD_DIM = 16, 4, 64
D_FF = -(-8 * D_MODEL // (3 * 256)) * 256

LANES = 128
MXU_N = 256
ROT_HALF = 16
assert MLA_ROPE == 2 * ROT_HALF and GQA_HEAD_DIM // 2 == 2 * ROT_HALF
LOG2E = math.log2(math.e)
VMEM_LIMIT = 56 * 1024 * 1024

F32 = jnp.float32
BF16 = jnp.bfloat16


def _cparams(sem):
    return pltpu.CompilerParams(dimension_semantics=sem, vmem_limit_bytes=VMEM_LIMIT)


def _const_spec(shape):
    nd = len(shape)
    return pl.BlockSpec(shape, lambda *_: (0,) * nd)


def _dot(a, b):
    return jnp.dot(a, b, preferred_element_type=F32)


def _dot_nt(a, b):
    return lax.dot_general(a, b, (((1,), (1,)), ((), ())), preferred_element_type=F32)


def _ada_kernel(c_ref, w_ref, b_ref, o_ref):
    c = c_ref[...]
    ca = c * (1.0 / (1.0 + jnp.exp(-c)))
    o_ref[0] = jnp.dot(ca, w_ref[0], preferred_element_type=F32,
                       precision=lax.Precision.HIGHEST) + b_ref[0]


def _ada_mod(c_all, ada_w, ada_b):
    rows = c_all.shape[0]
    depth, d, n = ada_w.shape
    tn = 1536
    return pl.pallas_call(
        _ada_kernel,
        grid=(depth, n // tn),
        in_specs=[pl.BlockSpec((rows, d), lambda l, j: (0, 0)),
                  pl.BlockSpec((1, d, tn), lambda l, j: (l, 0, j)),
                  pl.BlockSpec((1, 1, tn), lambda l, j: (l, 0, j))],
        out_specs=pl.BlockSpec((1, rows, tn), lambda l, j: (l, 0, j)),
        out_shape=jax.ShapeDtypeStruct((depth, rows, n), F32),
        compiler_params=_cparams(("arbitrary", "arbitrary")),
        name="ada_mod",
    )(c_all, ada_w, ada_b.reshape(depth, 1, n))


def _norm_mod(x, g, shift, scale):
    ms = jnp.mean(x * x, axis=-1, keepdims=True)
    y = x * lax.rsqrt(ms + EPS)
    return (y * g) * (1.0 + scale) + shift


def _group_rms(a, gmat, inv_n):
    ssq = _dot((a * a).astype(BF16), gmat)
    return a * lax.rsqrt(ssq * inv_n + EPS)


def _rope_chunk(a, cos, sin_signed, first_half):
    fwd = pltpu.roll(a, ROT_HALF, 1)
    bwd = pltpu.roll(a, LANES - ROT_HALF, 1)
    return a * cos + jnp.where(first_half, bwd, fwd) * sin_signed


def _first_half_mask():
    lane = lax.broadcasted_iota(jnp.int32, (1, LANES), 1)
    return (lane & ROT_HALF) == 0


def _qkv_kernel(*refs, n_q, n_k, n_v, q_const, rope):
    if rope:
        (x_ref, mod_ref, ng_ref, w_ref, gq_ref, gk_ref, gm_ref, cos_ref, sin_ref,
         q_ref, k_ref, v_ref) = refs
    else:
        (x_ref, mod_ref, ng_ref, w_ref, gq_ref, gk_ref, gm_ref,
         q_ref, k_ref, v_ref) = refs
    h = _norm_mod(x_ref[0], ng_ref[...], mod_ref[0, 0:1, :], mod_ref[0, 1:2, :])
    hb = h.astype(BF16)
    gmat = gm_ref[...]
    if rope:
        cos = cos_ref[...]
        sin = sin_ref[...]
        fh = _first_half_mask()

    def normed(col0, gain, const):
        a = _dot(hb, w_ref[:, col0:col0 + MXU_N])
        a = _group_rms(a, gmat, 1.0 / 64.0) * (gain * const)
        if rope:
            a = jnp.concatenate(
                [_rope_chunk(a[:, j * LANES:(j + 1) * LANES], cos, sin, fh)
                 for j in range(MXU_N // LANES)], axis=1)
        return a.astype(BF16)

    for c in range(n_q // MXU_N):
        sl = slice(c * MXU_N, (c + 1) * MXU_N)
        q_ref[0, :, sl] = normed(c * MXU_N, gq_ref[:, sl], q_const)
    for c in range(n_k // MXU_N):
        sl = slice(c * MXU_N, (c + 1) * MXU_N)
        k_ref[0, :, sl] = normed(n_q + c * MXU_N, gk_ref[:, sl], 1.0)
    for c in range(n_v // MXU_N):
        sl = slice(c * MXU_N, (c + 1) * MXU_N)
        v_ref[0, :, sl] = _dot(hb, w_ref[:, n_q + n_k + c * MXU_N:
                                          n_q + n_k + (c + 1) * MXU_N]).astype(BF16)


def _qkv_proj(x, mod, ng, w, gq, gk, gmat, n_q, n_k, n_v, q_const, rope_tabs=None, tm=512):
    b, s, d = x.shape
    n = n_q + n_k + n_v
    rope = rope_tabs is not None
    in_specs = [pl.BlockSpec((1, tm, d), lambda i, j: (i, j, 0)),
                pl.BlockSpec((1, 6, d), lambda i, j: (i, 0, 0)),
                _const_spec((1, d)),
                _const_spec((d, n)),
                _const_spec((1, n_q)),
                _const_spec((1, n_k)),
                _const_spec((MXU_N, MXU_N))]
    args = [x, mod, ng, w, gq, gk, gmat]
    if rope:
        in_specs += [pl.BlockSpec((tm, LANES), lambda i, j: (j, 0))] * 2
        args += list(rope_tabs)
    outs = [jax.ShapeDtypeStruct((b, s, m), BF16) for m in (n_q, n_k, n_v)]
    out_specs = [pl.BlockSpec((1, tm, m), lambda i, j: (i, j, 0)) for m in (n_q, n_k, n_v)]
    return pl.pallas_call(
        functools.partial(_qkv_kernel, n_q=n_q, n_k=n_k, n_v=n_v, q_const=q_const, rope=rope),
        grid=(b, s // tm),
        in_specs=in_specs,
        out_specs=out_specs,
        out_shape=outs,
        compiler_params=_cparams(("parallel", "parallel")),
        name="qkv_proj",
    )(*args)


MLA_LAT_PAD = 768
MLA_SLOT = 128


def _mla_proj_kernel(x_ref, mod_ref, ng_ref, win_ref, gql_ref, gkvl_ref, wq_ref, wk_ref, wv_ref,
                     gq_ref, gk_ref, gm_ref, cos_ref, sin_ref, q_ref, k_ref, v_ref, *, q_const):
    h = _norm_mod(x_ref[0], ng_ref[...], mod_ref[0, 0:1, :], mod_ref[0, 1:2, :])
    lat = _dot(h.astype(BF16), win_ref[...])
    q_lat = lat[:, :MLA_Q_LORA]
    kv_lat = lat[:, MLA_Q_LORA:MLA_Q_LORA + MLA_KV_LORA]
    k_rope = lat[:, MLA_Q_LORA + MLA_KV_LORA:]
    qn = q_lat * lax.rsqrt(jnp.mean(q_lat * q_lat, axis=-1, keepdims=True) + EPS) * gql_ref[...]
    kvn = kv_lat * lax.rsqrt(jnp.mean(kv_lat * kv_lat, axis=-1, keepdims=True) + EPS) * gkvl_ref[...]
    qnb = qn.astype(BF16)
    kvnb = kvn.astype(BF16)
    kvr = jnp.concatenate([kvnb, k_rope.astype(BF16)], axis=1)
    gmat = gm_ref[...]
    cos = cos_ref[...]
    sin = sin_ref[...]
    fh = _first_half_mask()
    inv_n = 1.0 / (MLA_NOPE + MLA_ROPE)
    n_slots = MLA_HEADS * MLA_SLOT

    def finish(a, gain, const):
        a = _group_rms(a, gmat, inv_n) * (gain * const)
        a = jnp.concatenate(
            [_rope_chunk(a[:, j * LANES:(j + 1) * LANES], cos, sin, fh)
             for j in range(MXU_N // LANES)], axis=1)
        return a.astype(BF16)

    for c in range(n_slots // MXU_N):
        sl = slice(c * MXU_N, (c + 1) * MXU_N)
        q_ref[0, :, sl] = finish(_dot(qnb, wq_ref[:, sl]), gq_ref[...], q_const)
        k_ref[0, :, sl] = finish(_dot(kvr, wk_ref[:, sl]), gk_ref[...], 1.0)
    for c in range(MLA_HEADS * MLA_V // MXU_N):
        sl = slice(c * MXU_N, (c + 1) * MXU_N)
        v_ref[0, :, sl] = _dot(kvnb, wv_ref[:, sl]).astype(BF16)


def _mla_proj(x, mod, ng, win, gql, gkvl, wq, wk, wv, gq, gk, gmat, cos, sin, q_const, tm=512):
    b, s, d = x.shape
    n_slots = MLA_HEADS * MLA_SLOT
    n_v = MLA_HEADS * MLA_V
    in_specs = [pl.BlockSpec((1, tm, d), lambda i, j: (i, j, 0)),
                pl.BlockSpec((1, 6, d), lambda i, j: (i, 0, 0)),
                _const_spec((1, d)),
                _const_spec(win.shape),
                _const_spec(gql.shape),
                _const_spec(gkvl.shape),
                _const_spec(wq.shape),
                _const_spec(wk.shape),
                _const_spec(wv.shape),
                _const_spec(gq.shape),
                _const_spec(gk.shape),
                _const_spec((MXU_N, MXU_N)),
                pl.BlockSpec((tm, LANES), lambda i, j: (j, 0)),
                pl.BlockSpec((tm, LANES), lambda i, j: (j, 0))]
    outs = [jax.ShapeDtypeStruct((b, s, m), BF16) for m in (n_slots, n_slots, n_v)]
    out_specs = [pl.BlockSpec((1, tm, m), lambda i, j: (i, j, 0)) for m in (n_slots, n_slots, n_v)]
    return pl.pallas_call(
        functools.partial(_mla_proj_kernel, q_const=q_const),
        grid=(b, s // tm),
        in_specs=in_specs,
        out_specs=out_specs,
        out_shape=outs,
        compiler_params=_cparams(("parallel", "parallel")),
        name="mla_proj",
    )(x, mod, ng, win, gql, gkvl, wq, wk, wv, gq, gk, gmat, cos, sin)


HEAD_LANES = 64


def _lo_mask():
    return lax.broadcasted_iota(jnp.int32, (1, LANES), 1) < HEAD_LANES


def _chunk_start(c, tkc):
    return c * tkc if isinstance(c, int) else pl.multiple_of(c * tkc, tkc)


def _flash_loop(*, n_chunks, n_streams, scores, pv_rhs, m_ref, acc_ref, p_refs, a_refs, fresh=None):
    lo = _lo_mask()

    def init():
        m_ref[...] = jnp.full(m_ref.shape, NEG_INF, F32)
        acc_ref[...] = jnp.zeros(acc_ref.shape, F32)

    _run_when(fresh, init)

    def softmax_stage(i, slot):
        for h in range(n_streams):
            s, shift = scores(i, h)
            m_prev = m_ref[h]
            row_max = jnp.max(s, axis=1, keepdims=True)
            if shift is not None:
                row_max = row_max + shift
            m_next = jnp.maximum(m_prev, row_max)
            a_refs[slot][h] = jnp.exp2(m_prev - m_next)
            m_ref[h] = m_next
            m_eff = m_next if shift is None else m_next - shift
            m_rep = jnp.concatenate([m_eff] * (s.shape[1] // LANES), axis=1)
            p_refs[slot][h] = jnp.exp2((s - m_rep).astype(BF16))

    def pv_stage(i, slot):
        pv = None
        for h in range(n_streams):
            t = _dot(p_refs[slot][h], pv_rhs(i, h))
            pv = t if pv is None else pv + t
        a = a_refs[slot][0]
        if n_streams == 2:
            a = jnp.where(lo, a, a_refs[slot][1])
        acc_ref[...] = acc_ref[...] * jnp.concatenate([a, a], axis=1) + pv

    def body(j, carry):
        softmax_stage(2 * j, 0)
        pv_stage(2 * j - 1, 1)
        softmax_stage(2 * j + 1, 1)
        pv_stage(2 * j, 0)
        return carry

    assert n_chunks % 2 == 0
    softmax_stage(0, 0)
    softmax_stage(1, 1)
    pv_stage(0, 0)
    lax.fori_loop(1, n_chunks // 2, body, 0, unroll=4)
    pv_stage(n_chunks - 1, 1)


FLASH_KEYS_PER_STEP = 8192


def _key_spans(s):
    return max(1, s // FLASH_KEYS_PER_STEP)


def _key_span_flags(n_spans):
    if n_spans == 1:
        return None, None
    span = pl.program_id(3)
    return span == 0, span == n_spans - 1


def _run_when(cond, fn):
    if cond is None:
        fn()
    else:
        pl.when(cond)(fn)


def _pair_pv_rhs(v, h):
    msk = _half_mask(v.shape, h)
    rhs = jnp.concatenate([jnp.where(msk, v.astype(F32), 0.0), jnp.where(msk, 1.0, 0.0)], axis=1)
    return rhs.astype(BF16)


def _half_mask(shape, h):
    lane = lax.broadcasted_iota(jnp.int32, shape, len(shape) - 1)
    return (lane < HEAD_LANES) if h == 0 else (lane >= HEAD_LANES)


def _mask_half(x, h):
    return jnp.where(_half_mask(x.shape, h), x.astype(F32), 0.0).astype(BF16)


def _flash_scratch(n_streams, m, tkc):
    return [pltpu.VMEM((n_streams, m, LANES), F32),
            pltpu.VMEM((m, 2 * LANES), F32),
            pltpu.VMEM((n_streams, m, tkc), BF16),
            pltpu.VMEM((n_streams, m, tkc), BF16),
            pltpu.VMEM((n_streams, m, LANES), F32),
            pltpu.VMEM((n_streams, m, LANES), F32)]


def _gqa_flash_kernel(q_ref, k_ref, v_ref, o_ref, m_ref, acc_ref, p0_ref, p1_ref, a0_ref, a1_ref,
                      *, tq, tkc, group, n_spans):
    q = q_ref[0]
    qs = [jnp.concatenate([_mask_half(q[:, g * LANES:(g + 1) * LANES], h)
                           for g in range(group)], axis=0)
          for h in range(2)]

    def scores(c, h):
        return _dot_nt(qs[h], k_ref[0, pl.ds(_chunk_start(c, tkc), tkc), :]), None

    def pv_rhs(c, h):
        return _pair_pv_rhs(v_ref[0, pl.ds(_chunk_start(c, tkc), tkc), :], h)

    first, last = _key_span_flags(n_spans)
    _flash_loop(n_chunks=k_ref.shape[1] // tkc, n_streams=2, scores=scores, pv_rhs=pv_rhs,
                m_ref=m_ref, acc_ref=acc_ref, p_refs=(p0_ref, p1_ref), a_refs=(a0_ref, a1_ref),
                fresh=first)

    def finalize():
        o = acc_ref[:, :LANES] / acc_ref[:, LANES:]
        for g in range(group):
            o_ref[0, :, g * LANES:(g + 1) * LANES] = o[g * tq:(g + 1) * tq].astype(BF16)

    _run_when(last, finalize)


def _gqa_flash(q, k, v, tq=256, tkc=512):
    b, s, nq = q.shape
    pairs = k.shape[2] // LANES
    group = nq // k.shape[2]
    wq = group * LANES
    tkc = min(tkc, s // 2)
    n_spans = _key_spans(s)
    span = s // n_spans
    return pl.pallas_call(
        functools.partial(_gqa_flash_kernel, tq=tq, tkc=tkc, group=group, n_spans=n_spans),
        grid=(b, pairs, s // tq, n_spans),
        in_specs=[pl.BlockSpec((1, tq, wq), lambda i, p, a, c: (i, a, p)),
                  pl.BlockSpec((1, span, LANES), lambda i, p, a, c: (i, c, p)),
                  pl.BlockSpec((1, span, LANES), lambda i, p, a, c: (i, c, p))],
        out_specs=pl.BlockSpec((1, tq, wq), lambda i, p, a, c: (i, a, p)),
        out_shape=jax.ShapeDtypeStruct((b, s, nq), BF16),
        scratch_shapes=_flash_scratch(2, group * tq, tkc),
        compiler_params=_cparams(("parallel", "parallel", "parallel", "arbitrary")),
        name="gqa_flash",
    )(q, k, v)


def _mla_flash_kernel(q_ref, k_ref, v_ref, o_ref, m_ref, acc_ref, p0_ref, p1_ref, a0_ref, a1_ref,
                      *, tkc, n_spans):
    qs = [q_ref[0, :, :LANES], q_ref[0, :, LANES:]]

    def scores(c, h):
        k = k_ref[0, pl.ds(_chunk_start(c, tkc), tkc), h * LANES:(h + 1) * LANES]
        return _dot_nt(qs[h], k), None

    def pv_rhs(c, h):
        return _pair_pv_rhs(v_ref[0, pl.ds(_chunk_start(c, tkc), tkc), :], h)

    first, last = _key_span_flags(n_spans)
    _flash_loop(n_chunks=k_ref.shape[1] // tkc, n_streams=2, scores=scores, pv_rhs=pv_rhs,
                m_ref=m_ref, acc_ref=acc_ref, p_refs=(p0_ref, p1_ref), a_refs=(a0_ref, a1_ref),
                fresh=first)

    def finalize():
        o_ref[0] = (acc_ref[:, :LANES] / acc_ref[:, LANES:]).astype(BF16)

    _run_when(last, finalize)


def _mla_flash(q, k, v, tq=1024, tkc=512):
    b, s, nv = v.shape
    pairs = nv // LANES
    tkc = min(tkc, s // 2)
    n_spans = _key_spans(s)
    span = s // n_spans
    return pl.pallas_call(
        functools.partial(_mla_flash_kernel, tkc=tkc, n_spans=n_spans),
        grid=(b, pairs, s // tq, n_spans),
        in_specs=[pl.BlockSpec((1, tq, 2 * LANES), lambda i, p, a, c: (i, a, p)),
                  pl.BlockSpec((1, span, 2 * LANES), lambda i, p, a, c: (i, c, p)),
                  pl.BlockSpec((1, span, LANES), lambda i, p, a, c: (i, c, p))],
        out_specs=pl.BlockSpec((1, tq, LANES), lambda i, p, a, c: (i, a, p)),
        out_shape=jax.ShapeDtypeStruct((b, s, nv), BF16),
        scratch_shapes=_flash_scratch(2, tq, tkc),
        compiler_params=_cparams(("parallel", "parallel", "parallel", "arbitrary")),
        name="mla_flash",
    )(q, k, v)


ALIBI_PIECES = 3
POS_SPLIT = 32


def _diff_flash_kernel(q_ref, k_ref, v_ref, kpos_ref, lam_ref, subg_ref, o_ref, m_ref, acc_ref,
                       p0_ref, p1_ref, a0_ref, a1_ref, qx_ref, row_ref, *, tq, lambda_init, n_spans):
    h = pl.program_id(1)
    d = pl.program_id(2)
    n = k_ref.shape[1] // tq
    q = q_ref[0]
    qs = jnp.concatenate([_mask_half(q, 0), _mask_half(q, 1)], axis=0)
    slope = jnp.exp2(-(8.0 / DIFF_HEADS) * (h + 1).astype(F32) * jnp.ones((1, 1), F32)) * LOG2E
    pieces = []
    rest = slope
    for _ in range(ALIBI_PIECES):
        piece = rest.astype(BF16).astype(F32)
        pieces.append(piece)
        rest = rest - piece
    lane = lax.broadcasted_iota(jnp.int32, (1, LANES), 1)
    aug = jnp.zeros((1, LANES), F32)
    for j, piece in enumerate(pieces):
        aug = jnp.where((lane == j) | (lane == j + ALIBI_PIECES), piece, aug)
    first, last = _key_span_flags(n_spans)

    def build_query_side():
        for side, sign in enumerate((1.0, -1.0)):
            aug_rows = jnp.broadcast_to(sign * aug, (2 * tq, LANES)).astype(BF16)
            qx_ref[side] = jnp.concatenate([qs, aug_rows], axis=1)
        r = lax.broadcasted_iota(jnp.int32, (2 * tq, LANES), 0) & (tq - 1)
        row_ref[...] = r.astype(F32) * slope

    _run_when(first, build_query_side)

    base = 0 if n_spans == 1 else pl.program_id(3) * n
    start = jnp.where((d >= base) & (d < base + n), d - base, 0)

    def local_chunk(i):
        return start if isinstance(i, int) and i == 0 else jnp.remainder(start + i, n)

    def scores(i, _):
        cl = local_chunk(i)
        c = base + cl
        k = k_ref[0, pl.ds(pl.multiple_of(cl * tq, tq), tq), :]
        if isinstance(i, int) and i == 0:
            rel = (lax.broadcasted_iota(jnp.int32, (tq, tq), 0)
                   - lax.broadcasted_iota(jnp.int32, (tq, tq), 1))
            bias = jnp.abs(rel + (d - c) * tq).astype(F32) * slope
            return _dot_nt(qs, k) - jnp.concatenate([bias, bias], axis=0), None
        before = c < d
        s = _dot_nt(qx_ref[jnp.where(before, 0, 1)], jnp.concatenate([k, kpos_ref[...]], axis=1))
        far = (jnp.abs(d - c) * tq).astype(F32)
        shift = -(slope * far) - jnp.where(before, 1.0, -1.0) * row_ref[...]
        return s, shift

    def pv_rhs(i, _):
        v = v_ref[0, pl.ds(pl.multiple_of(local_chunk(i) * tq, tq), tq), :]
        return jnp.concatenate([v, jnp.ones(v.shape, BF16)], axis=1)

    _flash_loop(n_chunks=n, n_streams=1, scores=scores, pv_rhs=pv_rhs,
                m_ref=m_ref, acc_ref=acc_ref, p_refs=(p0_ref, p1_ref), a_refs=(a0_ref, a1_ref),
                fresh=first)

    def finalize():
        lp = lam_ref[...]
        lam = (jnp.exp(jnp.sum(lp[0:1] * lp[1:2], axis=1, keepdims=True))
               - jnp.exp(jnp.sum(lp[2:3] * lp[3:4], axis=1, keepdims=True)) + lambda_init)
        o = acc_ref[:, :LANES] / acc_ref[:, LANES:]
        o = o[:tq] - lam * o[tq:]
        o = o * lax.rsqrt(jnp.mean(o * o, axis=-1, keepdims=True) + EPS) * subg_ref[...]
        o_ref[0] = (o * (1.0 - lambda_init)).astype(BF16)

    _run_when(last, finalize)


def _diff_kpos_table(tq):
    c = np.arange(tq)
    tab = np.zeros((tq, LANES), np.float32)
    tab[:, :ALIBI_PIECES] = (POS_SPLIT * (c // POS_SPLIT))[:, None]
    tab[:, ALIBI_PIECES:2 * ALIBI_PIECES] = (c % POS_SPLIT)[:, None]
    return jnp.asarray(tab, BF16)


def _diff_flash(q, k, v, lam_p, sub_g, lambda_init, tq=512):
    b, s, n = v.shape
    heads = n // LANES
    tq = min(tq, s // 2)
    assert tq & (tq - 1) == 0 and tq <= POS_SPLIT * 256
    n_spans = _key_spans(s)
    span = s // n_spans
    return pl.pallas_call(
        functools.partial(_diff_flash_kernel, tq=tq, lambda_init=lambda_init, n_spans=n_spans),
        grid=(b, heads, s // tq, n_spans),
        in_specs=[pl.BlockSpec((1, tq, LANES), lambda i, p, a, c: (i, a, p)),
                  pl.BlockSpec((1, span, LANES), lambda i, p, a, c: (i, c, p)),
                  pl.BlockSpec((1, span, LANES), lambda i, p, a, c: (i, c, p)),
                  _const_spec((tq, LANES)),
                  _const_spec(lam_p.shape),
                  _const_spec(sub_g.shape)],
        out_specs=pl.BlockSpec((1, tq, LANES), lambda i, p, a, c: (i, a, p)),
        out_shape=jax.ShapeDtypeStruct((b, s, n), BF16),
        scratch_shapes=_flash_scratch(1, 2 * tq, tq) + [pltpu.VMEM((2, 2 * tq, 2 * LANES), BF16),
                                                        pltpu.VMEM((2 * tq, LANES), F32)],
        compiler_params=_cparams(("parallel", "parallel", "parallel", "arbitrary")),
        name="diff_flash",
    )(q, k, v, _diff_kpos_table(tq), lam_p, sub_g)


NA_ROWS_PER_STEP = 16
NA_BLOCK = NA_ROWS_PER_STEP * GRID_W
NA_WIN = NA_KH * GRID_W


def _na_kernel(q_ref, kp_ref, kc_ref, kn_ref, vp_ref, vc_ref, vn_ref, bias_ref, o_ref,
               kw_ref, vw_ref, *, rows):
    blk = pl.program_id(2)
    kw_ref[0:NA_BLOCK] = kp_ref[0]
    kw_ref[NA_BLOCK:2 * NA_BLOCK] = kc_ref[0]
    kw_ref[2 * NA_BLOCK:] = kn_ref[0]
    vw_ref[0:NA_BLOCK] = vp_ref[0]
    vw_ref[NA_BLOCK:2 * NA_BLOCK] = vc_ref[0]
    vw_ref[2 * NA_BLOCK:] = vn_ref[0]
    lo = _lo_mask()
    for rl in range(NA_ROWS_PER_STEP):
        r = blk * NA_ROWS_PER_STEP + rl
        r0 = jnp.clip(r - NA_KH // 2, 0, rows - NA_KH)
        start = pl.multiple_of((r0 - blk * NA_ROWS_PER_STEP + NA_ROWS_PER_STEP) * GRID_W, GRID_W)
        off = r0 - r + (NA_KH - 1)
        kwin = kw_ref[pl.ds(start, NA_WIN), :]
        vwin = vw_ref[pl.ds(start, NA_WIN), :]
        qr = q_ref[0, rl * GRID_W:(rl + 1) * GRID_W, :]
        qs = jnp.concatenate([_mask_half(qr, 0), _mask_half(qr, 1)], axis=0)
        bias = jnp.concatenate(
            [jnp.concatenate([bias_ref[half, off + 2 * j] for j in range(NA_KH // 2)], axis=1)
             for half in range(2)], axis=0)
        s = _dot_nt(qs, kwin) + bias
        p = jnp.exp(s - jnp.max(s, axis=1, keepdims=True))
        o = _dot(p.astype(BF16), vwin) / jnp.sum(p, axis=1, keepdims=True)
        o_ref[0, rl * GRID_W:(rl + 1) * GRID_W, :] = jnp.where(lo, o[:GRID_W], o[GRID_W:]).astype(BF16)


def _na_attention(q, k, v, bias_tab):
    b, s, n = q.shape
    pairs = n // LANES
    rows = s // GRID_W
    nblk = s // NA_BLOCK
    blk_spec = lambda f: pl.BlockSpec((1, NA_BLOCK, LANES), f)
    prev = lambda i, p, a: (i, jnp.maximum(a - 1, 0), p)
    cur = lambda i, p, a: (i, a, p)
    nxt = lambda i, p, a: (i, jnp.minimum(a + 1, nblk - 1), p)
    return pl.pallas_call(
        functools.partial(_na_kernel, rows=rows),
        grid=(b, pairs, nblk),
        in_specs=[blk_spec(cur), blk_spec(prev), blk_spec(cur), blk_spec(nxt),
                  blk_spec(prev), blk_spec(cur), blk_spec(nxt),
                  pl.BlockSpec((2,) + bias_tab.shape[1:], lambda i, p, a: (p, 0, 0, 0))],
        out_specs=blk_spec(cur),
        out_shape=jax.ShapeDtypeStruct((b, s, n), BF16),
        scratch_shapes=[pltpu.VMEM((3 * NA_BLOCK, LANES), BF16),
                        pltpu.VMEM((3 * NA_BLOCK, LANES), BF16)],
        compiler_params=_cparams(("parallel", "parallel", "parallel")),
        name="na_attention",
    )(q, k, k, k, v, v, v, bias_tab)


def _na_bias_table(rpb):
    col = np.arange(GRID_W)
    col_start = np.clip(col - NA_KW // 2, 0, GRID_W - NA_KW)
    valid = (col[None, :] >= col_start[:, None]) & (col[None, :] < col_start[:, None] + NA_KW)
    dc = np.clip(col[None, :] - col[:, None] + NA_KW - 1, 0, 2 * NA_KW - 2)
    t = rpb.astype(F32)[:, :, dc]
    t = jnp.where(jnp.asarray(valid)[None, None], t, NEG_INF)
    return jnp.concatenate([t[:, :-1], t[:, 1:]], axis=-1)


FF_CHUNK = 256


def _post_kernel(x_ref, a_ref, mod_ref, wo_ref, ng_ref, w1_ref, w2_ref, o_ref):
    x = x_ref[0]
    x = x + mod_ref[0, 2:3, :] * _dot(a_ref[0], wo_ref[...])
    h = _norm_mod(x, ng_ref[...], mod_ref[0, 3:4, :], mod_ref[0, 4:5, :]).astype(BF16)
    y = None
    for c in range(D_FF // FF_CHUNK):
        g = _dot(h, w1_ref[:, c * FF_CHUNK:(c + 1) * FF_CHUNK])
        u = _dot(h, w1_ref[:, D_FF + c * FF_CHUNK:D_FF + (c + 1) * FF_CHUNK])
        act = (g * (1.0 / (1.0 + jnp.exp(-g))) * u).astype(BF16)
        part = _dot(act, w2_ref[c * FF_CHUNK:(c + 1) * FF_CHUNK, :])
        y = part if y is None else y + part
    o_ref[0] = x + mod_ref[0, 5:6, :] * y


def _post(x, attn, mod, wo, ng, w1, w2, tm=512):
    b, s, d = x.shape
    single = pl.Buffered(1)
    return pl.pallas_call(
        _post_kernel,
        grid=(b, s // tm),
        in_specs=[pl.BlockSpec((1, tm, d), lambda i, j: (i, j, 0)),
                  pl.BlockSpec((1, tm, attn.shape[2]), lambda i, j: (i, j, 0)),
                  pl.BlockSpec((1, 6, d), lambda i, j: (i, 0, 0)),
                  pl.BlockSpec(wo.shape, lambda i, j: (0, 0), pipeline_mode=single),
                  _const_spec((1, d)),
                  pl.BlockSpec(w1.shape, lambda i, j: (0, 0), pipeline_mode=single),
                  pl.BlockSpec(w2.shape, lambda i, j: (0, 0), pipeline_mode=single)],
        out_specs=pl.BlockSpec((1, tm, d), lambda i, j: (i, j, 0)),
        out_shape=jax.ShapeDtypeStruct((b, s, d), F32),
        compiler_params=_cparams(("parallel", "parallel")),
        name="post_ffn",
    )(x, attn, mod, wo, ng, w1, w2)


def _block_diag_ones(block):
    idx = np.arange(MXU_N) // block
    return jnp.asarray(idx[:, None] == idx[None, :], dtype=BF16)


def _rope_angles(pos, dim):
    inv = ROPE_THETA ** (-jnp.arange(0, dim, 2, dtype=F32) / dim)
    ang = pos.astype(F32)[:, None] * inv[None, :]
    return jnp.concatenate([ang, ang], axis=-1)


def _sin_signed(ang):
    half = ang.shape[-1] // 2
    sign = jnp.concatenate([-jnp.ones((half,), F32), jnp.ones((half,), F32)])
    return jnp.sin(ang) * sign


def _axial_tables(s):
    t = jnp.arange(s)
    half = GQA_HEAD_DIM // 2
    ar, ac = _rope_angles(t // GRID_W, half), _rope_angles(t % GRID_W, half)
    cos = jnp.concatenate([jnp.cos(ar), jnp.cos(ac)], axis=-1)
    sin = jnp.concatenate([_sin_signed(ar), _sin_signed(ac)], axis=-1)
    return jnp.tile(cos, (1, 2)), jnp.tile(sin, (1, 2))


def _mla_tables(s):
    ang = _rope_angles(jnp.arange(s), MLA_ROPE)
    ones = jnp.ones((s, MLA_NOPE), F32)
    zpad = jnp.zeros((s, MLA_SLOT - MLA_NOPE - MLA_ROPE), F32)
    cos = jnp.concatenate([ones, jnp.cos(ang), zpad], axis=-1)
    sin = jnp.concatenate([jnp.zeros_like(ones), _sin_signed(ang), zpad], axis=-1)
    return cos, sin


def _gqa_q_order():
    group = GQA_Q_HEADS // GQA_KV_HEADS
    order = []
    for p in range(GQA_KV_HEADS // 2):
        for g in range(group):
            order += [group * (2 * p) + g, group * (2 * p + 1) + g]
    return np.asarray(order)


def _head_cols(order, width):
    return (np.asarray(order)[:, None] * width + np.arange(width)[None, :]).reshape(-1)


def _mla_weights(w_in, w_q_up, w_kv_up):
    d = w_in.shape[0]
    win = jnp.concatenate([w_in, jnp.zeros((d, MLA_LAT_PAD - w_in.shape[1]), w_in.dtype)], axis=1)
    qd = MLA_NOPE + MLA_ROPE
    wq = w_q_up.reshape(MLA_Q_LORA, MLA_HEADS, qd)
    wq = jnp.concatenate([wq, jnp.zeros((MLA_Q_LORA, MLA_HEADS, MLA_SLOT - qd), wq.dtype)], axis=-1)
    wq = wq.reshape(MLA_Q_LORA, MLA_HEADS * MLA_SLOT)
    wkv = w_kv_up.reshape(MLA_KV_LORA, MLA_HEADS, MLA_NOPE + MLA_V)
    k_nope = jnp.concatenate(
        [wkv[..., :MLA_NOPE], jnp.zeros((MLA_KV_LORA, MLA_HEADS, MLA_SLOT - MLA_NOPE), wkv.dtype)], axis=-1)
    place = np.zeros((LANES, MLA_HEADS, MLA_SLOT), np.float32)
    for i in range(MLA_ROPE):
        place[i, :, MLA_NOPE + i] = 1.0
    wk = jnp.concatenate([k_nope.reshape(MLA_KV_LORA, -1),
                          jnp.asarray(place.reshape(LANES, -1), wkv.dtype)], axis=0)
    wv = wkv[..., MLA_NOPE:].reshape(MLA_KV_LORA, MLA_HEADS * MLA_V)
    return win.astype(BF16), wq.astype(BF16), wk.astype(BF16), wv.astype(BF16)


def _slot_gain(g):
    g = jnp.concatenate([g, jnp.zeros((MLA_SLOT - g.shape[0],), g.dtype)])
    return jnp.tile(g, MXU_N // MLA_SLOT)[None, :]


def _trunk(x, mods, p):
    s = x.shape[1]
    for l in range(DEPTH):
        mod = mods[l]
        ng1 = p['norm_g'][l, 0][None, :]
        ng2 = p['norm_g'][l, 1][None, :]
        kind, j = l % 4, l // 4
        if kind == 0:
            n = NA_HEADS * NA_HEAD_DIM
            q, k, v = _qkv_proj(x, mod, ng1, p['na_w_in'][j],
                                jnp.tile(p['na_qk_g'][j, 0], NA_HEADS)[None, :],
                                jnp.tile(p['na_qk_g'][j, 1], NA_HEADS)[None, :],
                                p['gm64'], n, n, n, NA_HEAD_DIM ** -0.5)
            a = _na_attention(q, k, v, p['na_bias'][j])
            wo = p['na_w_out'][j]
        elif kind == 1:
            win, wq, wk, wv = p['mla_w'][j]
            cos, sin = p['mla_tabs'][s]
            q, k, v = _mla_proj(x, mod, ng1, win, p['mla_q_lat_g'][j][None, :],
                                p['mla_kv_lat_g'][j][None, :], wq, wk, wv,
                                _slot_gain(p['mla_qk_g'][j, 0]), _slot_gain(p['mla_qk_g'][j, 1]),
                                p['gm128'], cos, sin, (MLA_NOPE + MLA_ROPE) ** -0.5 * LOG2E)
            a = _mla_flash(q, k, v)
            wo = p['mla_w_out'][j]
        elif kind == 2:
            n = DIFF_HEADS * 2 * DIFF_HEAD_DIM
            lambda_init = 0.8 - 0.6 * math.exp(-0.3 * l)
            q, k, v = _qkv_proj(x, mod, ng1, p['diff_w_in'][j],
                                jnp.tile(p['diff_qk_g'][j, 0], 2 * DIFF_HEADS)[None, :],
                                jnp.tile(p['diff_qk_g'][j, 1], 2 * DIFF_HEADS)[None, :],
                                p['gm64'], n, n, n, DIFF_HEAD_DIM ** -0.5 * LOG2E)
            a = _diff_flash(q, k, v, p['diff_lambda'][j], p['diff_sub_g'][j][None, :], lambda_init)
            wo = p['diff_w_out'][j]
        else:
            nq = GQA_Q_HEADS * GQA_HEAD_DIM
            nkv = GQA_KV_HEADS * GQA_HEAD_DIM
            q, k, v = _qkv_proj(x, mod, ng1, p['gqa_w_in'][j],
                                jnp.tile(p['gqa_qk_g'][j, 0], GQA_Q_HEADS)[None, :],
                                jnp.tile(p['gqa_qk_g'][j, 1], GQA_KV_HEADS)[None, :],
                                p['gm64'], nq, nkv, nkv, GQA_HEAD_DIM ** -0.5 * LOG2E,
                                rope_tabs=p['gqa_tabs'][s])
            a = _gqa_flash(q, k, v)
            wo = p['gqa_w_out'][j]
        x = _post(x, a, mod, wo, ng2, p['ffn_w_in'][l], p['ffn_w_out'][l])
    return x


def kernel(x_prompt, x_sample, c_prompt, c_sample, norm_g, ada_w, ada_b, na_w_in, na_qk_g, na_rpb, na_w_out, mla_w_in, mla_q_lat_g, mla_kv_lat_g, mla_w_q_up, mla_w_kv_up, mla_qk_g, mla_w_out, diff_w_in, diff_qk_g, diff_lambda, diff_sub_g, diff_w_out, gqa_w_in, gqa_qk_g, gqa_w_out, ffn_w_in, ffn_w_out):
    bp, bs = x_prompt.shape[0], x_sample.shape[0]
    d = x_prompt.shape[2]
    rows = -(-(bp + bs) // 8) * 8
    c_all = jnp.concatenate([c_prompt, c_sample, jnp.zeros((rows - bp - bs, d), F32)], axis=0)
    mods = _ada_mod(c_all, ada_w, ada_b)
    mods_p = mods[:, :bp].reshape(DEPTH, bp, 6, d)
    mods_s = mods[:, bp:bp + bs].reshape(DEPTH, bs, 6, d)

    q_cols = _head_cols(_gqa_q_order(), GQA_HEAD_DIM)
    nq = GQA_Q_HEADS * GQA_HEAD_DIM
    gqa_in = jnp.concatenate([gqa_w_in[:, :, :nq][:, :, q_cols], gqa_w_in[:, :, nq:]], axis=2)
    seqs = sorted({x_prompt.shape[1], x_sample.shape[1]})
    p = {
        'norm_g': norm_g,
        'gm64': _block_diag_ones(64), 'gm128': _block_diag_ones(128),
        'na_w_in': na_w_in.astype(BF16), 'na_qk_g': na_qk_g, 'na_w_out': na_w_out.astype(BF16),
        'na_bias': [_na_bias_table(na_rpb[j]) for j in range(na_rpb.shape[0])],
        'mla_w': [_mla_weights(mla_w_in[j], mla_w_q_up[j], mla_w_kv_up[j])
                  for j in range(mla_w_in.shape[0])],
        'mla_tabs': {s: _mla_tables(s) for s in seqs},
        'mla_q_lat_g': mla_q_lat_g, 'mla_kv_lat_g': mla_kv_lat_g, 'mla_qk_g': mla_qk_g,
        'mla_w_out': mla_w_out.astype(BF16),
        'diff_w_in': diff_w_in.astype(BF16), 'diff_qk_g': diff_qk_g, 'diff_lambda': diff_lambda,
        'diff_sub_g': diff_sub_g, 'diff_w_out': diff_w_out.astype(BF16),
        'gqa_w_in': gqa_in.astype(BF16), 'gqa_qk_g': gqa_qk_g,
        'gqa_w_out': gqa_w_out[:, q_cols, :].astype(BF16),
        'gqa_tabs': {s: _axial_tables(s) for s in seqs},
        'ffn_w_in': ffn_w_in.astype(BF16), 'ffn_w_out': ffn_w_out.astype(BF16),
    }
    return (_trunk(x_prompt, mods_p, p), _trunk(x_sample, mods_s, p))
```

```python
import functools
import math

import numpy as np
import jax
import jax.numpy as jnp
from jax import lax
from jax.experimental import pallas as pl
from jax.experimental.pallas import tpu as pltpu

D_MODEL = 1024
DEPTH = 4
GRID_W = 64
EPS = 1e-6
NEG_INF = -1e30
ROPE_THETA = 10000.0
NA_HEADS, NA_HEAD_DIM, NA_KH, NA_KW = 16, 64, 8, 16
MLA_HEADS, MLA_Q_LORA, MLA_KV_LORA, MLA_NOPE, MLA_ROPE, MLA_V = 16, 384, 256, 64, 32, 64
DIFF_HEADS, DIFF_HEAD_DIM = 8, 64
GQA_Q_HEADS, GQA_KV_HEADS, GQA_HEAD_DIM = 16, 4, 64
D_FF = -(-8 * D_MODEL // (3 * 256)) * 256

LANES = 128
MXU_N = 256
ROT_HALF = 16
assert MLA_ROPE == 2 * ROT_HALF and GQA_HEAD_DIM // 2 == 2 * ROT_HALF
LOG2E = math.log2(math.e)
VMEM_LIMIT = 56 * 1024 * 1024

F32 = jnp.float32
BF16 = jnp.bfloat16


def _cparams(sem):
    return pltpu.CompilerParams(dimension_semantics=sem, vmem_limit_bytes=VMEM_LIMIT)


def _const_spec(shape):
    nd = len(shape)
    return pl.BlockSpec(shape, lambda *_: (0,) * nd)


def _dot(a, b):
    return jnp.dot(a, b, preferred_element_type=F32)


def _dot_nt(a, b):
    return lax.dot_general(a, b, (((1,), (1,)), ((), ())), preferred_element_type=F32)


def _ada_kernel(c_ref, w_ref, b_ref, o_ref):
    c = c_ref[...]
    ca = c * (1.0 / (1.0 + jnp.exp(-c)))
    o_ref[0] = jnp.dot(ca, w_ref[0], preferred_element_type=F32,
                       precision=lax.Precision.HIGHEST) + b_ref[0]


def _ada_mod(c_all, ada_w, ada_b):
    rows = c_all.shape[0]
    depth, d, n = ada_w.shape
    tn = 1536
    return pl.pallas_call(
        _ada_kernel,
        grid=(depth, n // tn),
        in_specs=[pl.BlockSpec((rows, d), lambda l, j: (0, 0)),
                  pl.BlockSpec((1, d, tn), lambda l, j: (l, 0, j)),
                  pl.BlockSpec((1, 1, tn), lambda l, j: (l, 0, j))],
        out_specs=pl.BlockSpec((1, rows, tn), lambda l, j: (l, 0, j)),
        out_shape=jax.ShapeDtypeStruct((depth, rows, n), F32),
        compiler_params=_cparams(("arbitrary", "arbitrary")),
        name="ada_mod",
    )(c_all, ada_w, ada_b.reshape(depth, 1, n))


def _norm_mod(x, g, shift, scale):
    ms = jnp.mean(x * x, axis=-1, keepdims=True)
    y = x * lax.rsqrt(ms + EPS)
    return (y * g) * (1.0 + scale) + shift


def _group_rms(a, gmat, inv_n):
    ssq = _dot((a * a).astype(BF16), gmat)
    return a * lax.rsqrt(ssq * inv_n + EPS)


def _rope_chunk(a, cos, sin_signed, first_half):
    fwd = pltpu.roll(a, ROT_HALF, 1)
    bwd = pltpu.roll(a, LANES - ROT_HALF, 1)
    return a * cos + jnp.where(first_half, bwd, fwd) * sin_signed


def _first_half_mask():
    lane = lax.broadcasted_iota(jnp.int32, (1, LANES), 1)
    return (lane & ROT_HALF) == 0


def _qkv_kernel(*refs, n_q, n_k, n_v, q_const, rope):
    if rope:
        (x_ref, mod_ref, ng_ref, w_ref, gq_ref, gk_ref, gm_ref, cos_ref, sin_ref,
         q_ref, k_ref, v_ref) = refs
    else:
        (x_ref, mod_ref, ng_ref, w_ref, gq_ref, gk_ref, gm_ref,
         q_ref, k_ref, v_ref) = refs
    h = _norm_mod(x_ref[0], ng_ref[...], mod_ref[0, 0:1, :], mod_ref[0, 1:2, :])
    hb = h.astype(BF16)
    gmat = gm_ref[...]
    if rope:
        cos = cos_ref[...]
        sin = sin_ref[...]
        fh = _first_half_mask()

    def normed(col0, gain, const):
        a = _dot(hb, w_ref[:, col0:col0 + MXU_N])
        a = _group_rms(a, gmat, 1.0 / 64.0) * (gain * const)
        if rope:
            a = jnp.concatenate(
                [_rope_chunk(a[:, j * LANES:(j + 1) * LANES], cos, sin, fh)
                 for j in range(MXU_N // LANES)], axis=1)
        return a.astype(BF16)

    for c in range(n_q // MXU_N):
        sl = slice(c * MXU_N, (c + 1) * MXU_N)
        q_ref[0, :, sl] = normed(c * MXU_N, gq_ref[:, sl], q_const)
    for c in range(n_k // MXU_N):
        sl = slice(c * MXU_N, (c + 1) * MXU_N)
        k_ref[0, :, sl] = normed(n_q + c * MXU_N, gk_ref[:, sl], 1.0)
    for c in range(n_v // MXU_N):
        sl = slice(c * MXU_N, (c + 1) * MXU_N)
        v_ref[0, :, sl] = _dot(hb, w_ref[:, n_q + n_k + c * MXU_N:
                                          n_q + n_k + (c + 1) * MXU_N]).astype(BF16)


def _qkv_proj(x, mod, ng, w, gq, gk, gmat, n_q, n_k, n_v, q_const, rope_tabs=None, tm=512):
    b, s, d = x.shape
    n = n_q + n_k + n_v
    rope = rope_tabs is not None
    in_specs = [pl.BlockSpec((1, tm, d), lambda i, j: (i, j, 0)),
                pl.BlockSpec((1, 6, d), lambda i, j: (i, 0, 0)),
                _const_spec((1, d)),
                _const_spec((d, n)),
                _const_spec((1, n_q)),
                _const_spec((1, n_k)),
                _const_spec((MXU_N, MXU_N))]
    args = [x, mod, ng, w, gq, gk, gmat]
    if rope:
        in_specs += [pl.BlockSpec((tm, LANES), lambda i, j: (j, 0))] * 2
        args += list(rope_tabs)
    outs = [jax.ShapeDtypeStruct((b, s, m), BF16) for m in (n_q, n_k, n_v)]
    out_specs = [pl.BlockSpec((1, tm, m), lambda i, j: (i, j, 0)) for m in (n_q, n_k, n_v)]
    return pl.pallas_call(
        functools.partial(_qkv_kernel, n_q=n_q, n_k=n_k, n_v=n_v, q_const=q_const, rope=rope),
        grid=(b, s // tm),
        in_specs=in_specs,
        out_specs=out_specs,
        out_shape=outs,
        compiler_params=_cparams(("parallel", "parallel")),
        name="qkv_proj",
    )(*args)


MLA_LAT_PAD = 768
MLA_SLOT = 128


def _mla_proj_kernel(x_ref, mod_ref, ng_ref, win_ref, gql_ref, gkvl_ref, wq_ref, wk_ref, wv_ref,
                     gq_ref, gk_ref, gm_ref, cos_ref, sin_ref, q_ref, k_ref, v_ref, *, q_const):
    h = _norm_mod(x_ref[0], ng_ref[...], mod_ref[0, 0:1, :], mod_ref[0, 1:2, :])
    lat = _dot(h.astype(BF16), win_ref[...])
    q_lat = lat[:, :MLA_Q_LORA]
    kv_lat = lat[:, MLA_Q_LORA:MLA_Q_LORA + MLA_KV_LORA]
    k_rope = lat[:, MLA_Q_LORA + MLA_KV_LORA:]
    qn = q_lat * lax.rsqrt(jnp.mean(q_lat * q_lat, axis=-1, keepdims=True) + EPS) * gql_ref[...]
    kvn = kv_lat * lax.rsqrt(jnp.mean(kv_lat * kv_lat, axis=-1, keepdims=True) + EPS) * gkvl_ref[...]
    qnb = qn.astype(BF16)
    kvnb = kvn.astype(BF16)
    kvr = jnp.concatenate([kvnb, k_rope.astype(BF16)], axis=1)
    gmat = gm_ref[...]
    cos = cos_ref[...]
    sin = sin_ref[...]
    fh = _first_half_mask()
    inv_n = 1.0 / (MLA_NOPE + MLA_ROPE)
    n_slots = MLA_HEADS * MLA_SLOT

    def finish(a, gain, const):
        a = _group_rms(a, gmat, inv_n) * (gain * const)
        a = jnp.concatenate(
            [_rope_chunk(a[:, j * LANES:(j + 1) * LANES], cos, sin, fh)
             for j in range(MXU_N // LANES)], axis=1)
        return a.astype(BF16)

    for c in range(n_slots // MXU_N):
        sl = slice(c * MXU_N, (c + 1) * MXU_N)
        q_ref[0, :, sl] = finish(_dot(qnb, wq_ref[:, sl]), gq_ref[...], q_const)
        k_ref[0, :, sl] = finish(_dot(kvr, wk_ref[:, sl]), gk_ref[...], 1.0)
    for c in range(MLA_HEADS * MLA_V // MXU_N):
        sl = slice(c * MXU_N, (c + 1) * MXU_N)
        v_ref[0, :, sl] = _dot(kvnb, wv_ref[:, sl]).astype(BF16)


def _mla_proj(x, mod, ng, win, gql, gkvl, wq, wk, wv, gq, gk, gmat, cos, sin, q_const, tm=512):
    b, s, d = x.shape
    n_slots = MLA_HEADS * MLA_SLOT
    n_v = MLA_HEADS * MLA_V
    in_specs = [pl.BlockSpec((1, tm, d), lambda i, j: (i, j, 0)),
                pl.BlockSpec((1, 6, d), lambda i, j: (i, 0, 0)),
                _const_spec((1, d)),
                _const_spec(win.shape),
                _const_spec(gql.shape),
                _const_spec(gkvl.shape),
                _const_spec(wq.shape),
                _const_spec(wk.shape),
                _const_spec(wv.shape),
                _const_spec(gq.shape),
                _const_spec(gk.shape),
                _const_spec((MXU_N, MXU_N)),
                pl.BlockSpec((tm, LANES), lambda i, j: (j, 0)),
                pl.BlockSpec((tm, LANES), lambda i, j: (j, 0))]
    outs = [jax.ShapeDtypeStruct((b, s, m), BF16) for m in (n_slots, n_slots, n_v)]
    out_specs = [pl.BlockSpec((1, tm, m), lambda i, j: (i, j, 0)) for m in (n_slots, n_slots, n_v)]
    return pl.pallas_call(
        functools.partial(_mla_proj_kernel, q_const=q_const),
        grid=(b, s // tm),
        in_specs=in_specs,
        out_specs=out_specs,
        out_shape=outs,
        compiler_params=_cparams(("parallel", "parallel")),
        name="mla_proj",
    )(x, mod, ng, win, gql, gkvl, wq, wk, wv, gq, gk, gmat, cos, sin)


HEAD_LANES = 64


def _lo_mask():
    return lax.broadcasted_iota(jnp.int32, (1, LANES), 1) < HEAD_LANES


def _chunk_start(c, tkc):
    return c * tkc if isinstance(c, int) else pl.multiple_of(c * tkc, tkc)


def _flash_loop(*, n_chunks, n_streams, scores, pv_rhs, m_ref, acc_ref, p_refs, a_refs, fresh=None):
    lo = _lo_mask()

    if fresh is None:
        m_ref[...] = jnp.full(m_ref.shape, NEG_INF, F32)
        acc_ref[...] = jnp.zeros(acc_ref.shape, F32)
    else:
        m_ref[...] = jnp.where(fresh, NEG_INF, m_ref[...])
        acc_ref[...] = jnp.where(fresh, 0.0, acc_ref[...])

    def softmax_stage(i, slot):
        for h in range(n_streams):
            s, shift = scores(i, h)
            m_prev = m_ref[h]
            row_max = jnp.max(s, axis=1, keepdims=True)
            if shift is not None:
                row_max = row_max + shift
            m_next = jnp.maximum(m_prev, row_max)
            a_refs[slot][h] = jnp.exp2(m_prev - m_next)
            m_ref[h] = m_next
            m_eff = m_next if shift is None else m_next - shift
            m_rep = jnp.concatenate([m_eff] * (s.shape[1] // LANES), axis=1)
            p_refs[slot][h] = jnp.exp2((s - m_rep).astype(BF16))

    def pv_stage(i, slot):
        pv = None
        for h in range(n_streams):
            t = _dot(p_refs[slot][h], pv_rhs(i, h))
            pv = t if pv is None else pv + t
        a = a_refs[slot][0]
        if n_streams == 2:
            a = jnp.where(lo, a, a_refs[slot][1])
        acc_ref[...] = acc_ref[...] * jnp.concatenate([a, a], axis=1) + pv

    def body(j, carry):
        softmax_stage(2 * j, 0)
        pv_stage(2 * j - 1, 1)
        softmax_stage(2 * j + 1, 1)
        pv_stage(2 * j, 0)
        return carry

    assert n_chunks % 2 == 0
    softmax_stage(0, 0)
    softmax_stage(1, 1)
    pv_stage(0, 0)
    lax.fori_loop(1, n_chunks // 2, body, 0, unroll=4)
    pv_stage(n_chunks - 1, 1)


FLASH_KEYS_PER_STEP = 8192


def _key_spans(s):
    return max(1, s // FLASH_KEYS_PER_STEP)


def _key_span_flags(n_spans):
    if n_spans == 1:
        return None, None
    span = pl.program_id(3)
    return span == 0, span == n_spans - 1


def _run_when(cond, fn):
    if cond is None:
        fn()
    else:
        pl.when(cond)(fn)


def _pair_pv_rhs(v, h):
    msk = _half_mask(v.shape, h)
    rhs = jnp.concatenate([jnp.where(msk, v.astype(F32), 0.0), jnp.where(msk, 1.0, 0.0)], axis=1)
    return rhs.astype(BF16)


def _half_mask(shape, h):
    lane = lax.broadcasted_iota(jnp.int32, shape, len(shape) - 1)
    return (lane < HEAD_LANES) if h == 0 else (lane >= HEAD_LANES)


def _mask_half(x, h):
    return jnp.where(_half_mask(x.shape, h), x.astype(F32), 0.0).astype(BF16)


def _flash_scratch(n_streams, m, tkc):
    return [pltpu.VMEM((n_streams, m, LANES), F32),
            pltpu.VMEM((m, 2 * LANES), F32),
            pltpu.VMEM((n_streams, m, tkc), BF16),
            pltpu.VMEM((n_streams, m, tkc), BF16),
            pltpu.VMEM((n_streams, m, LANES), F32),
            pltpu.VMEM((n_streams, m, LANES), F32)]


def _gqa_flash_kernel(q_ref, k_ref, v_ref, o_ref, m_ref, acc_ref, p0_ref, p1_ref, a0_ref, a1_ref,
                      *, tq, tkc, group, n_spans):
    q = q_ref[0]
    qs = [jnp.concatenate([_mask_half(q[:, g * LANES:(g + 1) * LANES], h)
                           for g in range(group)], axis=0)
          for h in range(2)]

    def scores(c, h):
        return _dot_nt(qs[h], k_ref[0, pl.ds(_chunk_start(c, tkc), tkc), :]), None

    def pv_rhs(c, h):
        return _pair_pv_rhs(v_ref[0, pl.ds(_chunk_start(c, tkc), tkc), :], h)

    first, last = _key_span_flags(n_spans)
    _flash_loop(n_chunks=k_ref.shape[1] // tkc, n_streams=2, scores=scores, pv_rhs=pv_rhs,
                m_ref=m_ref, acc_ref=acc_ref, p_refs=(p0_ref, p1_ref), a_refs=(a0_ref, a1_ref),
                fresh=first)

    def finalize():
        o = acc_ref[:, :LANES] / acc_ref[:, LANES:]
        for g in range(group):
            o_ref[0, :, g * LANES:(g + 1) * LANES] = o[g * tq:(g + 1) * tq].astype(BF16)

    finalize()


def _gqa_flash(q, k, v, tq=256, tkc=512):
    b, s, nq = q.shape
    pairs = k.shape[2] // LANES
    group = nq // k.shape[2]
    wq = group * LANES
    tkc = min(tkc, s // 2)
    n_spans = _key_spans(s)
    span = s // n_spans
    return pl.pallas_call(
        functools.partial(_gqa_flash_kernel, tq=tq, tkc=tkc, group=group, n_spans=n_spans),
        grid=(b, pairs, s // tq, n_spans),
        in_specs=[pl.BlockSpec((1, tq, wq), lambda i, p, a, c: (i, a, p)),
                  pl.BlockSpec((1, span, LANES), lambda i, p, a, c: (i, c, p)),
                  pl.BlockSpec((1, span, LANES), lambda i, p, a, c: (i, c, p))],
        out_specs=pl.BlockSpec((1, tq, wq), lambda i, p, a, c: (i, a, p)),
        out_shape=jax.ShapeDtypeStruct((b, s, nq), BF16),
        scratch_shapes=_flash_scratch(2, group * tq, tkc),
        compiler_params=_cparams(("parallel", "parallel", "parallel", "arbitrary")),
        name="gqa_flash",
    )(q, k, v)


def _mla_flash_kernel(q_ref, k_ref, v_ref, o_ref, m_ref, acc_ref, p0_ref, p1_ref, a0_ref, a1_ref,
                      *, tkc, n_spans):
    qs = [q_ref[0, :, :LANES], q_ref[0, :, LANES:]]

    def scores(c, h):
        k = k_ref[0, pl.ds(_chunk_start(c, tkc), tkc), h * LANES:(h + 1) * LANES]
        return _dot_nt(qs[h], k), None

    def pv_rhs(c, h):
        return _pair_pv_rhs(v_ref[0, pl.ds(_chunk_start(c, tkc), tkc), :], h)

    first, last = _key_span_flags(n_spans)
    _flash_loop(n_chunks=k_ref.shape[1] // tkc, n_streams=2, scores=scores, pv_rhs=pv_rhs,
                m_ref=m_ref, acc_ref=acc_ref, p_refs=(p0_ref, p1_ref), a_refs=(a0_ref, a1_ref),
                fresh=first)

    def finalize():
        o_ref[0] = (acc_ref[:, :LANES] / acc_ref[:, LANES:]).astype(BF16)

    finalize()


def _mla_flash(q, k, v, tq=1024, tkc=512):
    b, s, nv = v.shape
    pairs = nv // LANES
    tkc = min(tkc, s // 2)
    n_spans = _key_spans(s)
    span = s // n_spans
    return pl.pallas_call(
        functools.partial(_mla_flash_kernel, tkc=tkc, n_spans=n_spans),
        grid=(b, pairs, s // tq, n_spans),
        in_specs=[pl.BlockSpec((1, tq, 2 * LANES), lambda i, p, a, c: (i, a, p)),
                  pl.BlockSpec((1, span, 2 * LANES), lambda i, p, a, c: (i, c, p)),
                  pl.BlockSpec((1, span, LANES), lambda i, p, a, c: (i, c, p))],
        out_specs=pl.BlockSpec((1, tq, LANES), lambda i, p, a, c: (i, a, p)),
        out_shape=jax.ShapeDtypeStruct((b, s, nv), BF16),
        scratch_shapes=_flash_scratch(2, tq, tkc),
        compiler_params=_cparams(("parallel", "parallel", "parallel", "arbitrary")),
        name="mla_flash",
    )(q, k, v)


ALIBI_PIECES = 3
POS_SPLIT = 32


def _diff_flash_kernel(q_ref, k_ref, v_ref, kpos_ref, lam_ref, subg_ref, o_ref, m_ref, acc_ref,
                       p0_ref, p1_ref, a0_ref, a1_ref, qx_ref, row_ref, *, tq, lambda_init, n_spans):
    h = pl.program_id(1)
    d = pl.program_id(2)
    n = k_ref.shape[1] // tq
    q = q_ref[0]
    qs = jnp.concatenate([_mask_half(q, 0), _mask_half(q, 1)], axis=0)
    slope = jnp.exp2(-(8.0 / DIFF_HEADS) * (h + 1).astype(F32) * jnp.ones((1, 1), F32)) * LOG2E
    pieces = []
    rest = slope
    for _ in range(ALIBI_PIECES):
        piece = rest.astype(BF16).astype(F32)
        pieces.append(piece)
        rest = rest - piece
    lane = lax.broadcasted_iota(jnp.int32, (1, LANES), 1)
    aug = jnp.zeros((1, LANES), F32)
    for j, piece in enumerate(pieces):
        aug = jnp.where((lane == j) | (lane == j + ALIBI_PIECES), piece, aug)
    first, last = _key_span_flags(n_spans)

    def build_query_side():
        for side, sign in enumerate((1.0, -1.0)):
            aug_rows = jnp.broadcast_to(sign * aug, (2 * tq, LANES)).astype(BF16)
            qx_ref[side] = jnp.concatenate([qs, aug_rows], axis=1)
        r = lax.broadcasted_iota(jnp.int32, (2 * tq, LANES), 0) & (tq - 1)
        row_ref[...] = r.astype(F32) * slope

    build_query_side()

    base = 0 if n_spans == 1 else pl.program_id(3) * n
    start = jnp.where((d >= base) & (d < base + n), d - base, 0)

    def local_chunk(i):
        return start if isinstance(i, int) and i == 0 else jnp.remainder(start + i, n)

    def scores(i, _):
        cl = local_chunk(i)
        c = base + cl
        k = k_ref[0, pl.ds(pl.multiple_of(cl * tq, tq), tq), :]
        if isinstance(i, int) and i == 0:
            rel = (lax.broadcasted_iota(jnp.int32, (tq, tq), 0)
                   - lax.broadcasted_iota(jnp.int32, (tq, tq), 1))
            bias = jnp.abs(rel + (d - c) * tq).astype(F32) * slope
            return _dot_nt(qs, k) - jnp.concatenate([bias, bias], axis=0), None
        before = c < d
        s = _dot_nt(qx_ref[jnp.where(before, 0, 1)], jnp.concatenate([k, kpos_ref[...]], axis=1))
        far = (jnp.abs(d - c) * tq).astype(F32)
        shift = -(slope * far) - jnp.where(before, 1.0, -1.0) * row_ref[...]
        return s, shift

    def pv_rhs(i, _):
        v = v_ref[0, pl.ds(pl.multiple_of(local_chunk(i) * tq, tq), tq), :]
        return jnp.concatenate([v, jnp.ones(v.shape, BF16)], axis=1)

    _flash_loop(n_chunks=n, n_streams=1, scores=scores, pv_rhs=pv_rhs,
                m_ref=m_ref, acc_ref=acc_ref, p_refs=(p0_ref, p1_ref), a_refs=(a0_ref, a1_ref),
                fresh=first)

    def finalize():
        lp = lam_ref[...]
        lam = (jnp.exp(jnp.sum(lp[0:1] * lp[1:2], axis=1, keepdims=True))
               - jnp.exp(jnp.sum(lp[2:3] * lp[3:4], axis=1, keepdims=True)) + lambda_init)
        o = acc_ref[:, :LANES] / acc_ref[:, LANES:]
        o = o[:tq] - lam * o[tq:]
        o = o * lax.rsqrt(jnp.mean(o * o, axis=-1, keepdims=True) + EPS) * subg_ref[...]
        o_ref[0] = (o * (1.0 - lambda_init)).astype(BF16)

    finalize()


def _diff_kpos_table(tq):
    c = np.arange(tq)
    tab = np.zeros((tq, LANES), np.float32)
    tab[:, :ALIBI_PIECES] = (POS_SPLIT * (c // POS_SPLIT))[:, None]
    tab[:, ALIBI_PIECES:2 * ALIBI_PIECES] = (c % POS_SPLIT)[:, None]
    return jnp.asarray(tab, BF16)


def _diff_flash(q, k, v, lam_p, sub_g, lambda_init, tq=512):
    b, s, n = v.shape
    heads = n // LANES
    tq = min(tq, s // 2)
    assert tq & (tq - 1) == 0 and tq <= POS_SPLIT * 256
    n_spans = _key_spans(s)
    span = s // n_spans
    return pl.pallas_call(
        functools.partial(_diff_flash_kernel, tq=tq, lambda_init=lambda_init, n_spans=n_spans),
        grid=(b, heads, s // tq, n_spans),
        in_specs=[pl.BlockSpec((1, tq, LANES), lambda i, p, a, c: (i, a, p)),
                  pl.BlockSpec((1, span, LANES), lambda i, p, a, c: (i, c, p)),
                  pl.BlockSpec((1, span, LANES), lambda i, p, a, c: (i, c, p)),
                  _const_spec((tq, LANES)),
                  _const_spec(lam_p.shape),
                  _const_spec(sub_g.shape)],
        out_specs=pl.BlockSpec((1, tq, LANES), lambda i, p, a, c: (i, a, p)),
        out_shape=jax.ShapeDtypeStruct((b, s, n), BF16),
        scratch_shapes=_flash_scratch(1, 2 * tq, tq) + [pltpu.VMEM((2, 2 * tq, 2 * LANES), BF16),
                                                        pltpu.VMEM((2 * tq, LANES), F32)],
        compiler_params=_cparams(("parallel", "parallel", "parallel", "arbitrary")),
        name="diff_flash",
    )(q, k, v, _diff_kpos_table(tq), lam_p, sub_g)


NA_ROWS_PER_STEP = 16
NA_BLOCK = NA_ROWS_PER_STEP * GRID_W
NA_WIN = NA_KH * GRID_W


def _na_kernel(q_ref, kp_ref, kc_ref, kn_ref, vp_ref, vc_ref, vn_ref, bias_ref, o_ref,
               kw_ref, vw_ref, *, rows):
    blk = pl.program_id(2)
    kw_ref[0:NA_BLOCK] = kp_ref[0]
    kw_ref[NA_BLOCK:2 * NA_BLOCK] = kc_ref[0]
    kw_ref[2 * NA_BLOCK:] = kn_ref[0]
    vw_ref[0:NA_BLOCK] = vp_ref[0]
    vw_ref[NA_BLOCK:2 * NA_BLOCK] = vc_ref[0]
    vw_ref[2 * NA_BLOCK:] = vn_ref[0]
    lo = _lo_mask()
    for rl in range(NA_ROWS_PER_STEP):
        r = blk * NA_ROWS_PER_STEP + rl
        r0 = jnp.clip(r - NA_KH // 2, 0, rows - NA_KH)
        start = pl.multiple_of((r0 - blk * NA_ROWS_PER_STEP + NA_ROWS_PER_STEP) * GRID_W, GRID_W)
        off = r0 - r + (NA_KH - 1)
        kwin = kw_ref[pl.ds(start, NA_WIN), :]
        vwin = vw_ref[pl.ds(start, NA_WIN), :]
        qr = q_ref[0, rl * GRID_W:(rl + 1) * GRID_W, :]
        qs = jnp.concatenate([_mask_half(qr, 0), _mask_half(qr, 1)], axis=0)
        bias = jnp.concatenate(
            [jnp.concatenate([bias_ref[half, off + 2 * j] for j in range(NA_KH // 2)], axis=1)
             for half in range(2)], axis=0)
        s = _dot_nt(qs, kwin) + bias
        p = jnp.exp(s - jnp.max(s, axis=1, keepdims=True))
        o = _dot(p.astype(BF16), vwin) / jnp.sum(p, axis=1, keepdims=True)
        o_ref[0, rl * GRID_W:(rl + 1) * GRID_W, :] = jnp.where(lo, o[:GRID_W], o[GRID_W:]).astype(BF16)


def _na_attention(q, k, v, bias_tab):
    b, s, n = q.shape
    pairs = n // LANES
    rows = s // GRID_W
    nblk = s // NA_BLOCK
    blk_spec = lambda f: pl.BlockSpec((1, NA_BLOCK, LANES), f)
    prev = lambda i, p, a: (i, jnp.maximum(a - 1, 0), p)
    cur = lambda i, p, a: (i, a, p)
    nxt = lambda i, p, a: (i, jnp.minimum(a + 1, nblk - 1), p)
    return pl.pallas_call(
        functools.partial(_na_kernel, rows=rows),
        grid=(b, pairs, nblk),
        in_specs=[blk_spec(cur), blk_spec(prev), blk_spec(cur), blk_spec(nxt),
                  blk_spec(prev), blk_spec(cur), blk_spec(nxt),
                  pl.BlockSpec((2,) + bias_tab.shape[1:], lambda i, p, a: (p, 0, 0, 0))],
        out_specs=blk_spec(cur),
        out_shape=jax.ShapeDtypeStruct((b, s, n), BF16),
        scratch_shapes=[pltpu.VMEM((3 * NA_BLOCK, LANES), BF16),
                        pltpu.VMEM((3 * NA_BLOCK, LANES), BF16)],
        compiler_params=_cparams(("parallel", "parallel", "parallel")),
        name="na_attention",
    )(q, k, k, k, v, v, v, bias_tab)


def _na_bias_table(rpb):
    col = np.arange(GRID_W)
    col_start = np.clip(col - NA_KW // 2, 0, GRID_W - NA_KW)
    valid = (col[None, :] >= col_start[:, None]) & (col[None, :] < col_start[:, None] + NA_KW)
    dc = np.clip(col[None, :] - col[:, None] + NA_KW - 1, 0, 2 * NA_KW - 2)
    t = rpb.astype(F32)[:, :, dc]
    t = jnp.where(jnp.asarray(valid)[None, None], t, NEG_INF)
    return jnp.concatenate([t[:, :-1], t[:, 1:]], axis=-1)


FF_CHUNK = 256


def _post_kernel(x_ref, a_ref, mod_ref, wo_ref, ng_ref, w1_ref, w2_ref, o_ref):
    x = x_ref[0]
    x = x + mod_ref[0, 2:3, :] * _dot(a_ref[0], wo_ref[...])
    h = _norm_mod(x, ng_ref[...], mod_ref[0, 3:4, :], mod_ref[0, 4:5, :]).astype(BF16)
    y = None
    for c in range(D_FF // FF_CHUNK):
        g = _dot(h, w1_ref[:, c * FF_CHUNK:(c + 1) * FF_CHUNK])
        u = _dot(h, w1_ref[:, D_FF + c * FF_CHUNK:D_FF + (c + 1) * FF_CHUNK])
        act = (g * (1.0 / (1.0 + jnp.exp(-g))) * u).astype(BF16)
        part = _dot(act, w2_ref[c * FF_CHUNK:(c + 1) * FF_CHUNK, :])
        y = part if y is None else y + part
    o_ref[0] = x + mod_ref[0, 5:6, :] * y


def _post(x, attn, mod, wo, ng, w1, w2, tm=512):
    b, s, d = x.shape
    single = pl.Buffered(1)
    return pl.pallas_call(
        _post_kernel,
        grid=(b, s // tm),
        in_specs=[pl.BlockSpec((1, tm, d), lambda i, j: (i, j, 0)),
                  pl.BlockSpec((1, tm, attn.shape[2]), lambda i, j: (i, j, 0)),
                  pl.BlockSpec((1, 6, d), lambda i, j: (i, 0, 0)),
                  pl.BlockSpec(wo.shape, lambda i, j: (0, 0), pipeline_mode=single),
                  _const_spec((1, d)),
                  pl.BlockSpec(w1.shape, lambda i, j: (0, 0), pipeline_mode=single),
                  pl.BlockSpec(w2.shape, lambda i, j: (0, 0), pipeline_mode=single)],
        out_specs=pl.BlockSpec((1, tm, d), lambda i, j: (i, j, 0)),
        out_shape=jax.ShapeDtypeStruct((b, s, d), F32),
        compiler_params=_cparams(("parallel", "parallel")),
        name="post_ffn",
    )(x, attn, mod, wo, ng, w1, w2)


def _block_diag_ones(block):
    idx = np.arange(MXU_N) // block
    return jnp.asarray(idx[:, None] == idx[None, :], dtype=BF16)


def _rope_angles(pos, dim):
    inv = ROPE_THETA ** (-jnp.arange(0, dim, 2, dtype=F32) / dim)
    ang = pos.astype(F32)[:, None] * inv[None, :]
    return jnp.concatenate([ang, ang], axis=-1)


def _sin_signed(ang):
    half = ang.shape[-1] // 2
    sign = jnp.concatenate([-jnp.ones((half,), F32), jnp.ones((half,), F32)])
    return jnp.sin(ang) * sign


def _axial_tables(s):
    t = jnp.arange(s)
    half = GQA_HEAD_DIM // 2
    ar, ac = _rope_angles(t // GRID_W, half), _rope_angles(t % GRID_W, half)
    cos = jnp.concatenate([jnp.cos(ar), jnp.cos(ac)], axis=-1)
    sin = jnp.concatenate([_sin_signed(ar), _sin_signed(ac)], axis=-1)
    return jnp.tile(cos, (1, 2)), jnp.tile(sin, (1, 2))


def _mla_tables(s):
    ang = _rope_angles(jnp.arange(s), MLA_ROPE)
    ones = jnp.ones((s, MLA_NOPE), F32)
    zpad = jnp.zeros((s, MLA_SLOT - MLA_NOPE - MLA_ROPE), F32)
    cos = jnp.concatenate([ones, jnp.cos(ang), zpad], axis=-1)
    sin = jnp.concatenate([jnp.zeros_like(ones), _sin_signed(ang), zpad], axis=-1)
    return cos, sin


def _gqa_q_order():
    group = GQA_Q_HEADS // GQA_KV_HEADS
    order = []
    for p in range(GQA_KV_HEADS // 2):
        for g in range(group):
            order += [group * (2 * p) + g, group * (2 * p + 1) + g]
    return np.asarray(order)


def _head_cols(order, width):
    return (np.asarray(order)[:, None] * width + np.arange(width)[None, :]).reshape(-1)


def _mla_weights(w_in, w_q_up, w_kv_up):
    d = w_in.shape[0]
    win = jnp.concatenate([w_in, jnp.zeros((d, MLA_LAT_PAD - w_in.shape[1]), w_in.dtype)], axis=1)
    qd = MLA_NOPE + MLA_ROPE
    wq = w_q_up.reshape(MLA_Q_LORA, MLA_HEADS, qd)
    wq = jnp.concatenate([wq, jnp.zeros((MLA_Q_LORA, MLA_HEADS, MLA_SLOT - qd), wq.dtype)], axis=-1)
    wq = wq.reshape(MLA_Q_LORA, MLA_HEADS * MLA_SLOT)
    wkv = w_kv_up.reshape(MLA_KV_LORA, MLA_HEADS, MLA_NOPE + MLA_V)
    k_nope = jnp.concatenate(
        [wkv[..., :MLA_NOPE], jnp.zeros((MLA_KV_LORA, MLA_HEADS, MLA_SLOT - MLA_NOPE), wkv.dtype)], axis=-1)
    place = np.zeros((LANES, MLA_HEADS, MLA_SLOT), np.float32)
    for i in range(MLA_ROPE):
        place[i, :, MLA_NOPE + i] = 1.0
    wk = jnp.concatenate([k_nope.reshape(MLA_KV_LORA, -1),
                          jnp.asarray(place.reshape(LANES, -1), wkv.dtype)], axis=0)
    wv = wkv[..., MLA_NOPE:].reshape(MLA_KV_LORA, MLA_HEADS * MLA_V)
    return win.astype(BF16), wq.astype(BF16), wk.astype(BF16), wv.astype(BF16)


def _slot_gain(g):
    g = jnp.concatenate([g, jnp.zeros((MLA_SLOT - g.shape[0],), g.dtype)])
    return jnp.tile(g, MXU_N // MLA_SLOT)[None, :]


def _trunk(x, mods, p):
    s = x.shape[1]
    for l in range(DEPTH):
        mod = mods[l]
        ng1 = p['norm_g'][l, 0][None, :]
        ng2 = p['norm_g'][l, 1][None, :]
        kind, j = l % 4, l // 4
        if kind == 0:
            n = NA_HEADS * NA_HEAD_DIM
            q, k, v = _qkv_proj(x, mod, ng1, p['na_w_in'][j],
                                jnp.tile(p['na_qk_g'][j, 0], NA_HEADS)[None, :],
                                jnp.tile(p['na_qk_g'][j, 1], NA_HEADS)[None, :],
                                p['gm64'], n, n, n, NA_HEAD_DIM ** -0.5)
            a = _na_attention(q, k, v, p['na_bias'][j])
            wo = p['na_w_out'][j]
        elif kind == 1:
            win, wq, wk, wv = p['mla_w'][j]
            cos, sin = p['mla_tabs'][s]
            q, k, v = _mla_proj(x, mod, ng1, win, p['mla_q_lat_g'][j][None, :],
                                p['mla_kv_lat_g'][j][None, :], wq, wk, wv,
                                _slot_gain(p['mla_qk_g'][j, 0]), _slot_gain(p['mla_qk_g'][j, 1]),
                                p['gm128'], cos, sin, (MLA_NOPE + MLA_ROPE) ** -0.5 * LOG2E)
            a = _mla_flash(q, k, v)
            wo = p['mla_w_out'][j]
        elif kind == 2:
            n = DIFF_HEADS * 2 * DIFF_HEAD_DIM
            lambda_init = 0.8 - 0.6 * math.exp(-0.3 * l)
            q, k, v = _qkv_proj(x, mod, ng1, p['diff_w_in'][j],
                                jnp.tile(p['diff_qk_g'][j, 0], 2 * DIFF_HEADS)[None, :],
                                jnp.tile(p['diff_qk_g'][j, 1], 2 * DIFF_HEADS)[None, :],
                                p['gm64'], n, n, n, DIFF_HEAD_DIM ** -0.5 * LOG2E)
            a = _diff_flash(q, k, v, p['diff_lambda'][j], p['diff_sub_g'][j][None, :], lambda_init)
            wo = p['diff_w_out'][j]
        else:
            nq = GQA_Q_HEADS * GQA_HEAD_DIM
            nkv = GQA_KV_HEADS * GQA_HEAD_DIM
            q, k, v = _qkv_proj(x, mod, ng1, p['gqa_w_in'][j],
                                jnp.tile(p['gqa_qk_g'][j, 0], GQA_Q_HEADS)[None, :],
                                jnp.tile(p['gqa_qk_g'][j, 1], GQA_KV_HEADS)[None, :],
                                p['gm64'], nq, nkv, nkv, GQA_HEAD_DIM ** -0.5 * LOG2E,
                                rope_tabs=p['gqa_tabs'][s])
            a = _gqa_flash(q, k, v)
            wo = p['gqa_w_out'][j]
        x = _post(x, a, mod, wo, ng2, p['ffn_w_in'][l], p['ffn_w_out'][l])
    return x


def kernel(x_prompt, x_sample, c_prompt, c_sample, norm_g, ada_w, ada_b, na_w_in, na_qk_g, na_rpb, na_w_out, mla_w_in, mla_q_lat_g, mla_kv_lat_g, mla_w_q_up, mla_w_kv_up, mla_qk_g, mla_w_out, diff_w_in, diff_qk_g, diff_lambda, diff_sub_g, diff_w_out, gqa_w_in, gqa_qk_g, gqa_w_out, ffn_w_in, ffn_w_out):
    bp, bs = x_prompt.shape[0], x_sample.shape[0]
    d = x_prompt.shape[2]
    rows = -(-(bp + bs) // 8) * 8
    c_all = jnp.concatenate([c_prompt, c_sample, jnp.zeros((rows - bp - bs, d), F32)], axis=0)
    mods = _ada_mod(c_all, ada_w, ada_b)
    mods_p = mods[:, :bp].reshape(DEPTH, bp, 6, d)
    mods_s = mods[:, bp:bp + bs].reshape(DEPTH, bs, 6, d)

    q_cols = _head_cols(_gqa_q_order(), GQA_HEAD_DIM)
    nq = GQA_Q_HEADS * GQA_HEAD_DIM
    gqa_in = jnp.concatenate([gqa_w_in[:, :, :nq][:, :, q_cols], gqa_w_in[:, :, nq:]], axis=2)
    seqs = sorted({x_prompt.shape[1], x_sample.shape[1]})
    p = {
        'norm_g': norm_g,
        'gm64': _block_diag_ones(64), 'gm128': _block_diag_ones(128),
        'na_w_in': na_w_in.astype(BF16), 'na_qk_g': na_qk_g, 'na_w_out': na_w_out.astype(BF16),
        'na_bias': [_na_bias_table(na_rpb[j]) for j in range(na_rpb.shape[0])],
        'mla_w': [_mla_weights(mla_w_in[j], mla_w_q_up[j], mla_w_kv_up[j])
                  for j in range(mla_w_in.shape[0])],
        'mla_tabs': {s: _mla_tables(s) for s in seqs},
        'mla_q_lat_g': mla_q_lat_g, 'mla_kv_lat_g': mla_kv_lat_g, 'mla_qk_g': mla_qk_g,
        'mla_w_out': mla_w_out.astype(BF16),
        'diff_w_in': diff_w_in.astype(BF16), 'diff_qk_g': diff_qk_g, 'diff_lambda': diff_lambda,
        'diff_sub_g': diff_sub_g, 'diff_w_out': diff_w_out.astype(BF16),
        'gqa_w_in': gqa_in.astype(BF16), 'gqa_qk_g': gqa_qk_g,
        'gqa_w_out': gqa_w_out[:, q_cols, :].astype(BF16),
        'gqa_tabs': {s: _axial_tables(s) for s in seqs},
        'ffn_w_in': ffn_w_in.astype(BF16), 'ffn_w_out': ffn_w_out.astype(BF16),
    }
    return (_trunk(x_prompt, mods_p, p), _trunk(x_sample, mods_s, p))
```
